```python
import jax, jax.numpy as jnp
from jax import lax
import numpy as np

D_MODEL = 2048
BATCH = 16
SEQ = 2048
DEPTH = 4
DEC_BATCH = 2
DEC_SEQ = 8192
PAST_LEN = 128

D_MIX = D_MODEL
W_FOURIER = D_MIX // 4
W_CONFORMER = D_MIX // 4
W_ATTN = D_MIX // 4
W_SHORTCONV = D_MIX - W_FOURIER - W_CONFORMER - W_ATTN
N_FOURIER_GROUPS = 4
FOURIER_GROUP = W_FOURIER // N_FOURIER_GROUPS
CONF_KERNEL = 31
HEAD_DIM = 64
N_ATTN_HEADS = W_ATTN // HEAD_DIM
DILATED_CONFIGS = ((128, 1), (512, 4), (2048, 16))
SHORT_KERNEL = 3
D_FF = 5504
RMS_EPS = 1e-6
LN_EPS = 1e-5
NEG_BIG = -1e30
IN_SIZES = (W_FOURIER, W_CONFORMER, W_CONFORMER, W_ATTN, W_ATTN, W_ATTN,
            W_SHORTCONV, W_SHORTCONV, W_SHORTCONV)
IN_COLS = sum(IN_SIZES)
IN_SPLITS = tuple(int(i) for i in np.cumsum(IN_SIZES)[:-1])

kernel_name = "hybrid_parallel_fourier_conformer_dilated_shortconv_encoder"


def rms_norm(x, g):
    xf = x.astype(jnp.float32)
    y = xf * lax.rsqrt(jnp.mean(xf * xf, axis=-1, keepdims=True) + RMS_EPS)
    return (y * g.astype(jnp.float32)).astype(x.dtype)


def layer_norm(x, g, b):
    xf = x.astype(jnp.float32)
    mu = jnp.mean(xf, axis=-1, keepdims=True)
    var = jnp.mean(jnp.square(xf - mu), axis=-1, keepdims=True)
    y = (xf - mu) * lax.rsqrt(var + LN_EPS) * g.astype(jnp.float32) + b.astype(jnp.float32)
    return y.astype(x.dtype)


def swiglu(x, wg, wu, wd):
    return (jax.nn.silu(x @ wg) * (x @ wu)) @ wd


def depthwise_conv(x, w):
    k = w.shape[0]
    c = x.shape[-1]
    return lax.conv_general_dilated(
        x, w.astype(x.dtype)[:, None, :], window_strides=(1,),
        padding=[(k // 2, k // 2)], dimension_numbers=("NWC", "WIO", "NWC"),
        feature_group_count=c)


def fourier_mix(u, w_f):
    b, s, _ = u.shape
    uf = u.astype(jnp.float32).reshape(b, s, N_FOURIER_GROUPS, FOURIER_GROUP)
    f = jnp.fft.fft2(uf, axes=(1, 3), norm="ortho").real
    y = jnp.einsum("bsgc,gce->bsge", f, w_f.astype(jnp.float32))
    return y.reshape(b, s, W_FOURIER).astype(u.dtype)


def conformer_conv(val, gate, w_dw, b_dw, ln_g, ln_b):
    h = val * jax.nn.sigmoid(gate)
    h = depthwise_conv(h, w_dw) + b_dw.astype(h.dtype)
    h = layer_norm(h, ln_g, ln_b)
    return jax.nn.silu(h)


def dilated_branch(q, k, v, window, dil, slopes):
    b, s, h, e = q.shape
    half = window // (2 * dil)
    l = s // dil
    nb = -(-l // half)
    lp = nb * half
    q_r = q.reshape(b, l, dil, h, e)
    k_r = k.reshape(b, l, dil, h, e)
    v_r = v.reshape(b, l, dil, h, e)
    zpad = ((0, 0), (0, 0), (0, 0))
    q_r = jnp.pad(q_r, ((0, 0), (0, lp - l)) + zpad)
    k_r = jnp.pad(k_r, ((0, 0), (half, lp - l + half)) + zpad)
    v_r = jnp.pad(v_r, ((0, 0), (half, lp - l + half)) + zpad)
    qb = q_r.reshape(b, nb, half, dil, h, e)
    kb = k_r.reshape(b, nb + 2, half, dil, h, e)
    vb = v_r.reshape(b, nb + 2, half, dil, h, e)
    kw = jnp.concatenate([kb[:, :-2], kb[:, 1:-1], kb[:, 2:]], axis=2)
    vw = jnp.concatenate([vb[:, :-2], vb[:, 1:-1], vb[:, 2:]], axis=2)
    sc = jnp.einsum("bnqrhe,bnkrhe->bnrhqk", qb, kw) * (e ** -0.5)
    offs = jnp.arange(3 * half)[None, :] - half - jnp.arange(half)[:, None]
    key_pos = jnp.arange(nb)[:, None] * half + jnp.arange(3 * half)[None, :] - half
    valid = (jnp.abs(offs) <= half)[None] & ((key_pos >= 0) & (key_pos < l))[:, None, :]
    alibi = -slopes[:, None, None] * (dil * jnp.abs(offs)).astype(jnp.float32)[None]
    sc = jnp.where(valid[None, :, None, None], sc + alibi[None, None, None], NEG_BIG)
    m = jnp.max(sc, axis=-1)
    p = jnp.exp(sc - m[..., None])
    den = jnp.sum(p, axis=-1)
    num = jnp.einsum("bnrhqk,bnkrhe->bnqrhe", p, vw)
    num = num.reshape(b, lp, dil, h, e)[:, :l].reshape(b, s, h, e)
    m = jnp.transpose(m, (0, 1, 4, 2, 3)).reshape(b, lp, dil, h)[:, :l].reshape(b, s, h)
    den = jnp.transpose(den, (0, 1, 4, 2, 3)).reshape(b, lp, dil, h)[:, :l].reshape(b, s, h)
    return num, m, den


def dilated_attention(q, k, v):
    b, s, _ = q.shape
    dt = q.dtype
    qh = q.astype(jnp.float32).reshape(b, s, N_ATTN_HEADS, HEAD_DIM)
    kh = k.astype(jnp.float32).reshape(b, s, N_ATTN_HEADS, HEAD_DIM)
    vh = v.astype(jnp.float32).reshape(b, s, N_ATTN_HEADS, HEAD_DIM)
    slopes = jnp.exp2(-8.0 * (jnp.arange(N_ATTN_HEADS, dtype=jnp.float32) + 1.0) / N_ATTN_HEADS)
    parts = [dilated_branch(qh, kh, vh, w, d, slopes) for (w, d) in DILATED_CONFIGS]
    m_all = jnp.max(jnp.stack([pm for (_, pm, _) in parts], axis=0), axis=0)
    num = sum(jnp.exp(pm - m_all)[..., None] * pn for (pn, pm, _) in parts)
    den = sum(jnp.exp(pm - m_all) * pd for (_, pm, pd) in parts)
    o = num / den[..., None]
    return o.reshape(b, s, W_ATTN).astype(dt)


def token_mixing(h, w_in, w_fourier, conv_b_w, conv_b_bias, ln_conv_gain, ln_conv_bias, conv_d_w, w_out):
    u = h @ w_in
    ua, cv, cg, q, k, v, sb, sc, sx = jnp.split(u, IN_SPLITS, axis=-1)
    y_a = fourier_mix(ua, w_fourier)
    y_b = conformer_conv(cv, cg, conv_b_w, conv_b_bias, ln_conv_gain, ln_conv_bias)
    y_c = dilated_attention(q, k, v)
    y_d = sb * depthwise_conv(sc * sx, conv_d_w)
    y = jnp.concatenate([y_a, y_b, y_c, y_d], axis=-1)
    return y @ w_out


def trunk(x, ln_ffn1, w_ffn1_gate, w_ffn1_up, w_ffn1_down, ln_mix, w_in, w_fourier, conv_b_w,
          conv_b_bias, ln_conv_gain, ln_conv_bias, conv_d_w, w_out, ln_ffn2, w_ffn2_gate,
          w_ffn2_up, w_ffn2_down, ln_final):
    for i in range(DEPTH):
        x = x + 0.5 * swiglu(rms_norm(x, ln_ffn1[i]), w_ffn1_gate[i], w_ffn1_up[i], w_ffn1_down[i])
        x = x + token_mixing(rms_norm(x, ln_mix[i]), w_in[i], w_fourier[i], conv_b_w[i], conv_b_bias[i],
                             ln_conv_gain[i], ln_conv_bias[i], conv_d_w[i], w_out[i])
        x = x + 0.5 * swiglu(rms_norm(x, ln_ffn2[i]), w_ffn2_gate[i], w_ffn2_up[i], w_ffn2_down[i])
    return rms_norm(x, ln_final)


def setup_inputs(seed: int = 0) -> dict:
    key = jax.random.key(seed)
    ks = jax.random.split(key, 22)
    nrm = jax.random.normal
    f32 = jnp.float32
    def gain(k, shape):
        return 1.0 + 0.02 * nrm(k, shape, f32)
    return {
        "x_prompt": nrm(ks[0], (BATCH, SEQ, D_MODEL), f32),
        "x_sample": nrm(ks[1], (DEC_BATCH, DEC_SEQ, D_MODEL), f32),
        "ln_ffn1": gain(ks[2], (DEPTH, D_MODEL)),
        "w_ffn1_gate": nrm(ks[3], (DEPTH, D_MODEL, D_FF), f32) * D_MODEL ** -0.5,
        "w_ffn1_up": nrm(ks[4], (DEPTH, D_MODEL, D_FF), f32) * D_MODEL ** -0.5,
        "w_ffn1_down": nrm(ks[5], (DEPTH, D_FF, D_MODEL), f32) * D_FF ** -0.5,
        "ln_mix": gain(ks[6], (DEPTH, D_MODEL)),
        "w_in": nrm(ks[7], (DEPTH, D_MODEL, IN_COLS), f32) * D_MODEL ** -0.5,
        "w_fourier": nrm(ks[8], (DEPTH, N_FOURIER_GROUPS, FOURIER_GROUP, FOURIER_GROUP), f32) * FOURIER_GROUP ** -0.5,
        "conv_b_w": nrm(ks[9], (DEPTH, CONF_KERNEL, W_CONFORMER), f32) * CONF_KERNEL ** -0.5,
        "conv_b_bias": 0.02 * nrm(ks[10], (DEPTH, W_CONFORMER), f32),
        "ln_conv_gain": gain(ks[11], (DEPTH, W_CONFORMER)),
        "ln_conv_bias": 0.02 * nrm(ks[12], (DEPTH, W_CONFORMER), f32),
        "conv_d_w": nrm(ks[13], (DEPTH, SHORT_KERNEL, W_SHORTCONV), f32) * SHORT_KERNEL ** -0.5,
        "w_out": nrm(ks[14], (DEPTH, D_MIX, D_MODEL), f32) * D_MIX ** -0.5,
        "ln_ffn2": gain(ks[15], (DEPTH, D_MODEL)),
        "w_ffn2_gate": nrm(ks[16], (DEPTH, D_MODEL, D_FF), f32) * D_MODEL ** -0.5,
        "w_ffn2_up": nrm(ks[17], (DEPTH, D_MODEL, D_FF), f32) * D_MODEL ** -0.5,
        "w_ffn2_down": nrm(ks[18], (DEPTH, D_FF, D_MODEL), f32) * D_FF ** -0.5,
        "ln_final": gain(ks[19], (D_MODEL,)),
    }


def reference(x_prompt, x_sample, ln_ffn1, w_ffn1_gate, w_ffn1_up, w_ffn1_down, ln_mix, w_in, w_fourier,
              conv_b_w, conv_b_bias, ln_conv_gain, ln_conv_bias, conv_d_w, w_out, ln_ffn2, w_ffn2_gate,
              w_ffn2_up, w_ffn2_down, ln_final):
    y_prompt = trunk(x_prompt, ln_ffn1, w_ffn1_gate, w_ffn1_up, w_ffn1_down, ln_mix, w_in, w_fourier,
                     conv_b_w, conv_b_bias, ln_conv_gain, ln_conv_bias, conv_d_w, w_out, ln_ffn2,
                     w_ffn2_gate, w_ffn2_up, w_ffn2_down, ln_final)
    y_sample = trunk(x_sample, ln_ffn1, w_ffn1_gate, w_ffn1_up, w_ffn1_down, ln_mix, w_in, w_fourier,
                     conv_b_w, conv_b_bias, ln_conv_gain, ln_conv_bias, conv_d_w, w_out, ln_ffn2,
                     w_ffn2_gate, w_ffn2_up, w_ffn2_down, ln_final)
    return (y_prompt, y_sample)
```

```python
import functools
import math

import jax
import jax.numpy as jnp
from jax import lax
from jax.experimental import pallas as pl
from jax.experimental.pallas import tpu as pltpu

F32 = jnp.float32
BF16 = jnp.bfloat16

N_FOURIER_GROUPS = 4
CONF_KERNEL = 31
SHORT_KERNEL = 3
HEAD_DIM = 64
DILATED_CONFIGS = ((128, 1), (512, 4), (2048, 16))
RMS_EPS = 1e-6
LN_EPS = 1e-5
NEG_BIG = -1e30

LANES = 128
SUBLANES_BF16 = 16
VMEM_LIMIT_BYTES = 60 * 1024 * 1024

FFN_TM = 1024
FFN_TF = 512
PROJ_TM = 512
DFT_TM = 1024
DFT_TK = 2048
DFT_BB = 2
DFT_GEN_ROWS = 128
CONV_TS = 512
CONV_HALO = 16
CONV_RC = 64
ATT_T = 256
NORM_TM = 1024


def _params(*sem):
    return pltpu.CompilerParams(dimension_semantics=sem, vmem_limit_bytes=VMEM_LIMIT_BYTES)


def _rms(x, gain):
    y = x * lax.rsqrt(jnp.mean(x * x, axis=-1, keepdims=True) + RMS_EPS)
    return y * gain


def _sigmoid(x):
    return 1.0 / (1.0 + jnp.exp(-x))


def _ffn_body(layer_ref, x_ref, gain_ref, wgu_ref, wd_ref, o_ref, xn_ref, *, tf):
    j = pl.program_id(1)

    @pl.when(j == 0)
    def _():
        x = x_ref[...]
        xn_ref[...] = _rms(x, gain_ref[...]).astype(BF16)
        o_ref[...] = x

    gu = jnp.dot(xn_ref[...], wgu_ref[...], preferred_element_type=F32)
    g = gu[:, :tf]
    u = gu[:, tf:]
    h = (g * _sigmoid(g)) * u * 0.5
    o_ref[...] += jnp.dot(h.astype(BF16), wd_ref[...], preferred_element_type=F32)


def _ffn(layer, x, gain, wgu, wd):
    t, d = x.shape
    tf = FFN_TF
    nf = wd.shape[1] // tf
    tm = FFN_TM
    return pl.pallas_call(
        functools.partial(_ffn_body, tf=tf),
        grid_spec=pltpu.PrefetchScalarGridSpec(
            num_scalar_prefetch=1,
            grid=(t // tm, nf),
            in_specs=[
                pl.BlockSpec((tm, d), lambda i, j, l: (i, 0), pipeline_mode=pl.Buffered(1)),
                pl.BlockSpec((None, 1, d), lambda i, j, l: (l[0], 0, 0)),
                pl.BlockSpec((None, d, 2 * tf), lambda i, j, l: (l[0], 0, j)),
                pl.BlockSpec((None, tf, d), lambda i, j, l: (l[0], j, 0)),
            ],
            out_specs=pl.BlockSpec((tm, d), lambda i, j, l: (i, 0)),
            scratch_shapes=[pltpu.VMEM((tm, d), BF16)],
        ),
        out_shape=jax.ShapeDtypeStruct((t, d), F32),
        compiler_params=_params("parallel", "arbitrary"),
        name="ffn",
    )(layer, x, gain, wgu, wd)


def _fold_body(w_ref, o_ref):
    n = w_ref.shape[0]
    r = lax.broadcasted_iota(jnp.int32, (n, n), 0)
    c = lax.broadcasted_iota(jnp.int32, (n, n), 1)
    k = (r * c) & (n - 1)
    k = jnp.where(k >= n // 2, k - n, k)
    ang = k.astype(F32) * (2.0 * math.pi / n)
    w = w_ref[...]
    scale = n ** -0.5
    a = jnp.dot(jnp.cos(ang), w, preferred_element_type=F32, precision=lax.Precision.HIGHEST)
    b = jnp.dot(jnp.sin(ang), w, preferred_element_type=F32, precision=lax.Precision.HIGHEST)
    o_ref[:, :n] = (a * scale).astype(BF16)
    o_ref[:, n:] = (b * -scale).astype(BF16)


def _fold_fourier(w_fourier):
    nl, ng, c, _ = w_fourier.shape
    return pl.pallas_call(
        _fold_body,
        grid=(nl, ng),
        in_specs=[pl.BlockSpec((None, None, c, c), lambda l, g: (l, g, 0, 0))],
        out_specs=pl.BlockSpec((None, None, c, 2 * c), lambda l, g: (l, g, 0, 0)),
        out_shape=jax.ShapeDtypeStruct((nl, ng, c, 2 * c), BF16),
        compiler_params=_params("parallel", "parallel"),
        name="fold_fourier",
    )(w_fourier)


def _dft_gen_body(cos_ref, sin_ref, *, s, rows):
    i = pl.program_id(0)
    r = lax.broadcasted_iota(jnp.int32, (rows, s), 0) + i * rows
    c = lax.broadcasted_iota(jnp.int32, (rows, s), 1)
    k = (r * c) & (s - 1)
    k = jnp.where(k >= s // 2, k - s, k)
    ang = k.astype(F32) * (2.0 * math.pi / s)
    cos_ref[...] = jnp.cos(ang).astype(BF16)
    sin_ref[...] = jnp.sin(ang).astype(BF16)


def _dft_tables(s):
    assert s & (s - 1) == 0 and (s - 1) ** 2 < 2 ** 31
    rows = DFT_GEN_ROWS
    spec = pl.BlockSpec((rows, s), lambda i: (i, 0))
    return pl.pallas_call(
        functools.partial(_dft_gen_body, s=s, rows=rows),
        grid=(s // rows,),
        out_specs=[spec, spec],
        out_shape=[jax.ShapeDtypeStruct((s, s), BF16)] * 2,
        compiler_params=_params("parallel"),
        name="dft_tables",
    )()


def _dft_body(c_ref, s_ref, v1_ref, v2_ref, o_ref, acc_ref, *, bb, scale):
    kk = pl.program_id(2)

    @pl.when(kk == 0)
    def _():
        acc_ref[...] = jnp.zeros_like(acc_ref)

    for b in range(bb):
        acc_ref[b] += (jnp.dot(c_ref[...], v1_ref[b], preferred_element_type=F32)
                       + jnp.dot(s_ref[...], v2_ref[b], preferred_element_type=F32))

    @pl.when(kk == pl.num_programs(2) - 1)
    def _():
        o_ref[...] = (acc_ref[...] * scale).astype(BF16)


def _dft(cos_t, sin_t, v1, v2):
    b, s, w = v1.shape
    bb, tm, tk = DFT_BB, min(DFT_TM, s), min(DFT_TK, s)
    tab = pl.BlockSpec((tm, tk), lambda n, i, k: (i, k))
    vin = pl.BlockSpec((bb, tk, w), lambda n, i, k: (n, k, 0))
    return pl.pallas_call(
        functools.partial(_dft_body, bb=bb, scale=s ** -0.5),
        grid=(b // bb, s // tm, s // tk),
        in_specs=[tab, tab, vin, vin],
        out_specs=pl.BlockSpec((bb, tm, w), lambda n, i, k: (n, i, 0)),
        out_shape=jax.ShapeDtypeStruct((b, s, w), BF16),
        scratch_shapes=[pltpu.VMEM((bb, tm, w), F32)],
        compiler_params=_params("parallel", "parallel", "arbitrary"),
        name="dft",
    )(cos_t, sin_t, v1, v2)


def _in_proj_body(layer_ref, x_ref, gain_ref, w_ref, fold_ref,
                  v1_ref, v2_ref, glu_ref, q_ref, k_ref, v_ref, sb_ref, sg_ref, *, w, ng):
    xn = _rms(x_ref[...], gain_ref[...]).astype(BF16)

    def proj(g):
        return jnp.dot(xn, w_ref[:, g * w:(g + 1) * w], preferred_element_type=F32)

    ua = proj(0).astype(BF16)
    c = w // ng
    for g in range(ng):
        vv = jnp.dot(ua[:, g * c:(g + 1) * c], fold_ref[g], preferred_element_type=F32)
        v1_ref[:, g * c:(g + 1) * c] = vv[:, :c].astype(BF16)
        v2_ref[:, g * c:(g + 1) * c] = vv[:, c:].astype(BF16)
    glu_ref[...] = proj(1) * _sigmoid(proj(2))
    q_ref[...] = (proj(3) * HEAD_DIM ** -0.5).astype(BF16)
    k_ref[...] = proj(4).astype(BF16)
    v_ref[...] = proj(5).astype(BF16)
    sb_ref[...] = proj(6)
    sg_ref[...] = proj(7) * proj(8)


def _in_proj(layer, x, gain, w_in, fold):
    t, d = x.shape
    ng, c = fold.shape[1], fold.shape[2]
    w = ng * c
    tm = PROJ_TM
    out = pl.BlockSpec((tm, w), lambda i, l: (i, 0))
    dts = (BF16, BF16, F32, BF16, BF16, BF16, F32, F32)
    return pl.pallas_call(
        functools.partial(_in_proj_body, w=w, ng=ng),
        grid_spec=pltpu.PrefetchScalarGridSpec(
            num_scalar_prefetch=1,
            grid=(t // tm,),
            in_specs=[
                pl.BlockSpec((tm, d), lambda i, l: (i, 0)),
                pl.BlockSpec((None, 1, d), lambda i, l: (l[0], 0, 0)),
                pl.BlockSpec((None, d, w_in.shape[2]), lambda i, l: (l[0], 0, 0),
                             pipeline_mode=pl.Buffered(1)),
                pl.BlockSpec((None, ng, c, 2 * c), lambda i, l: (l[0], 0, 0, 0)),
            ],
            out_specs=[out] * len(dts),
        ),
        out_shape=[jax.ShapeDtypeStruct((t, w), dt) for dt in dts],
        compiler_params=_params("parallel"),
        name="in_proj",
    )(layer, x, gain, w_in, fold)


def _conv_body(layer_ref, h_ref, hp_ref, hn_ref, g_ref, gp_ref, gn_ref, sb_ref,
               wb_ref, bias_ref, lng_ref, lnb_ref, wd_ref, yb_ref, yd_ref, hbuf, gbuf, *, ts):
    i = pl.program_id(1)
    first = i == 0
    last = i == pl.num_programs(1) - 1
    halo = CONV_HALO
    for buf, prev, cur, nxt in ((hbuf, hp_ref, h_ref, hn_ref), (gbuf, gp_ref, g_ref, gn_ref)):
        buf[0:halo] = jnp.where(first, 0.0, prev[...])
        buf[halo:halo + ts] = cur[...]
        buf[halo + ts:] = jnp.where(last, 0.0, nxt[...])

    kb, kd = CONF_KERNEL, SHORT_KERNEL
    for r0 in range(0, ts, CONV_RC):
        rows = slice(r0, r0 + CONV_RC)
        acc = jnp.zeros((CONV_RC, h_ref.shape[-1]), F32) + bias_ref[...]
        for k in range(kb):
            s0 = r0 + halo - kb // 2 + k
            acc = acc + wb_ref[k:k + 1, :] * hbuf[s0:s0 + CONV_RC, :]
        mu = jnp.mean(acc, axis=-1, keepdims=True)
        cen = acc - mu
        var = jnp.mean(cen * cen, axis=-1, keepdims=True)
        y = cen * lax.rsqrt(var + LN_EPS) * lng_ref[...] + lnb_ref[...]
        yb_ref[rows, :] = (y * _sigmoid(y)).astype(BF16)

        acd = jnp.zeros((CONV_RC, g_ref.shape[-1]), F32)
        for k in range(kd):
            s0 = r0 + halo - kd // 2 + k
            acd = acd + wd_ref[k:k + 1, :] * gbuf[s0:s0 + CONV_RC, :]
        yd_ref[rows, :] = (sb_ref[rows, :] * acd).astype(BF16)


def _conv(layer, glu, sg, sb, conv_b_w, conv_b_bias, ln_g, ln_b, conv_d_w):
    b, s, w = glu.shape
    ts, halo = CONV_TS, CONV_HALO
    nh = ts // halo
    main = pl.BlockSpec((None, ts, w), lambda n, i, l: (n, i, 0))
    prev = pl.BlockSpec((None, halo, w), lambda n, i, l: (n, jnp.maximum(i * nh - 1, 0), 0))
    nxt = pl.BlockSpec((None, halo, w), lambda n, i, l: (n, jnp.minimum((i + 1) * nh, s // halo - 1), 0))

    def per_layer(a):
        return pl.BlockSpec((None,) + a.shape[1:], lambda n, i, l: (l[0], 0, 0))

    return pl.pallas_call(
        functools.partial(_conv_body, ts=ts),
        grid_spec=pltpu.PrefetchScalarGridSpec(
            num_scalar_prefetch=1,
            grid=(b, s // ts),
            in_specs=[main, prev, nxt, main, prev, nxt, main,
                      per_layer(conv_b_w), per_layer(conv_b_bias), per_layer(ln_g), per_layer(ln_b),
                      per_layer(conv_d_w)],
            out_specs=[main, main],
            scratch_shapes=[pltpu.VMEM((ts + 2 * halo, w), F32)] * 2,
        ),
        out_shape=[jax.ShapeDtypeStruct((b, s, w), BF16)] * 2,
        compiler_params=_params("parallel", "parallel"),
        name="conv",
    )(layer, glu, glu, glu, sg, sg, sg, sb, conv_b_w, conv_b_bias, ln_g, ln_b, conv_d_w)


def _attn_body(q_ref, k_ref, v_ref, o_ref, m_ref, l_ref, acc_ref, *, t, nblk, reach, nh):
    i = pl.program_id(1)
    j = pl.program_id(2)

    @pl.when(j == 0)
    def _():
        m_ref[...] = jnp.full_like(m_ref, NEG_BIG)
        l_ref[...] = jnp.zeros_like(l_ref)
        acc_ref[...] = jnp.zeros_like(acc_ref)

    jb = i + j - reach

    @pl.when(jnp.logical_and(jb >= 0, jb < nblk))
    def _():
        d = ((reach - j) * t + lax.broadcasted_iota(jnp.int32, (t, t), 0)
             - lax.broadcasted_iota(jnp.int32, (t, t), 1))
        ad = jnp.abs(d)
        cnt = jnp.zeros((t, t), F32)
        for window, dil in DILATED_CONFIGS:
            hit = jnp.logical_and(ad <= window // 2, (d & (dil - 1)) == 0)
            cnt = cnt + hit.astype(F32)
        live = cnt > 0.0
        dist = ad.astype(F32)
        e = HEAD_DIM
        for h in range(nh):
            cols = slice(h * e, (h + 1) * e)
            s = lax.dot_general(q_ref[:, cols], k_ref[:, cols], (((1,), (1,)), ((), ())),
                                preferred_element_type=F32)
            slope = 2.0 ** (-8.0 * (h + 1) / nh)
            s = jnp.where(live, s - slope * dist, NEG_BIG)
            m_old = m_ref[h]
            m_new = jnp.maximum(m_old, jnp.max(s, axis=-1, keepdims=True))
            p = jnp.exp(s - m_new) * cnt
            alpha = jnp.exp(m_old - m_new)
            l_ref[h] = alpha * l_ref[h] + jnp.sum(p, axis=-1, keepdims=True)
            acc_ref[:, cols] = alpha * acc_ref[:, cols] + jnp.dot(
                p.astype(BF16), v_ref[:, cols], preferred_element_type=F32)
            m_ref[h] = m_new

    @pl.when(j == pl.num_programs(2) - 1)
    def _():
        e = HEAD_DIM
        for h in range(nh):
            cols = slice(h * e, (h + 1) * e)
            o_ref[:, cols] = (acc_ref[:, cols] / l_ref[h]).astype(BF16)


def _attn(q, k, v):
    b, s, w = q.shape
    t = ATT_T
    nblk = s // t
    for window, dil in DILATED_CONFIGS:
        assert dil & (dil - 1) == 0 and s % dil == 0 and (window // (2 * dil)) * dil == window // 2
    reach = max(-(-(win // 2) // t) for win, _ in DILATED_CONFIGS)
    nh = w // HEAD_DIM
    qs = pl.BlockSpec((None, t, w), lambda n, i, j: (n, i, 0))
    ks = pl.BlockSpec((None, t, w), lambda n, i, j: (n, jnp.clip(i + j - reach, 0, nblk - 1), 0))
    return pl.pallas_call(
        functools.partial(_attn_body, t=t, nblk=nblk, reach=reach, nh=nh),
        grid=(b, nblk, 2 * reach + 1),
        in_specs=[qs, ks, ks],
        out_specs=qs,
        out_shape=jax.ShapeDtypeStruct((b, s, w), BF16),
        scratch_shapes=[pltpu.VMEM((nh, t, 1), F32), pltpu.VMEM((nh, t, 1), F32),
                        pltpu.VMEM((t, w), F32)],
        compiler_params=_params("parallel", "parallel", "arbitrary"),
        name="attn",
    )(q, k, v)


def _out_proj_body(layer_ref, x_ref, ya_ref, yb_ref, yc_ref, yd_ref, w_ref, o_ref, *, w):
    acc = x_ref[...]
    for g, y_ref in enumerate((ya_ref, yb_ref, yc_ref, yd_ref)):
        acc = acc + jnp.dot(y_ref[...], w_ref[g * w:(g + 1) * w, :], preferred_element_type=F32)
    o_ref[...] = acc


def _out_proj(layer, x, ya, yb, yc, yd, w_out):
    t, d = x.shape
    w = ya.shape[1]
    tm = PROJ_TM
    xs = pl.BlockSpec((tm, d), lambda i, l: (i, 0))
    ys = pl.BlockSpec((tm, w), lambda i, l: (i, 0))
    return pl.pallas_call(
        functools.partial(_out_proj_body, w=w),
        grid_spec=pltpu.PrefetchScalarGridSpec(
            num_scalar_prefetch=1,
            grid=(t // tm,),
            in_specs=[xs, ys, ys, ys, ys,
                      pl.BlockSpec((None,) + w_out.shape[1:], lambda i, l: (l[0], 0, 0),
                                   pipeline_mode=pl.Buffered(1))],
            out_specs=xs,
        ),
        out_shape=jax.ShapeDtypeStruct((t, d), F32),
        compiler_params=_params("parallel"),
        name="out_proj",
    )(layer, x, ya, yb, yc, yd, w_out)


def _norm_body(x_ref, gain_ref, o_ref):
    o_ref[...] = _rms(x_ref[...], gain_ref[...])


def _final_norm(x, gain):
    t, d = x.shape
    tm = NORM_TM
    xs = pl.BlockSpec((tm, d), lambda i: (i, 0))
    return pl.pallas_call(
        _norm_body,
        grid=(t // tm,),
        in_specs=[xs, pl.BlockSpec((1, d), lambda i: (0, 0))],
        out_specs=xs,
        out_shape=jax.ShapeDtypeStruct((t, d), F32),
        compiler_params=_params("parallel"),
        name="final_norm",
    )(x, gain)


def _prep_ffn(wg, wu, wd):
    nl, d, f = wg.shape
    tf = FFN_TF
    nf = -(-f // tf)
    pad = nf * tf - f
    wg = jnp.pad(wg.astype(BF16), ((0, 0), (0, 0), (0, pad))).reshape(nl, d, nf, tf)
    wu = jnp.pad(wu.astype(BF16), ((0, 0), (0, 0), (0, pad))).reshape(nl, d, nf, tf)
    wgu = jnp.concatenate([wg, wu], axis=3).reshape(nl, d, nf * 2 * tf)
    wd = jnp.pad(wd.astype(BF16), ((0, 0), (0, pad), (0, 0)))
    return wgu, wd


def _trunk(x, p, n_layers):
    b, s, d = x.shape
    cos_t, sin_t = _dft_tables(s)

    def seq(a):
        return a.reshape(b, s, a.shape[-1])

    def layer_fn(li, x2):
        layer = jnp.full((1,), li, jnp.int32)
        x2 = _ffn(layer, x2, p["ln_ffn1"], p["wgu1"], p["wd1"])
        v1, v2, glu, q, k, v, sb, sg = _in_proj(layer, x2, p["ln_mix"], p["w_in"], p["fold"])
        ya = _dft(cos_t, sin_t, seq(v1), seq(v2))
        yb, yd = _conv(layer, seq(glu), seq(sg), seq(sb), p["conv_b_w"], p["conv_b_bias"],
                       p["ln_conv_gain"], p["ln_conv_bias"], p["conv_d_w"])
        yc = _attn(seq(q), seq(k), seq(v))
        flat = lambda a: a.reshape(b * s, a.shape[-1])
        x2 = _out_proj(layer, x2, flat(ya), flat(yb), flat(yc), flat(yd), p["w_out"])
        return _ffn(layer, x2, p["ln_ffn2"], p["wgu2"], p["wd2"])

    x2 = lax.fori_loop(0, n_layers, layer_fn, x.reshape(b * s, d))
    return _final_norm(x2, p["ln_final"]).reshape(b, s, d)


def _prep_params(ln_ffn1, w_ffn1_gate, w_ffn1_up, w_ffn1_down, ln_mix, w_in, w_fourier, conv_b_w,
                 conv_b_bias, ln_conv_gain, ln_conv_bias, conv_d_w, w_out, ln_ffn2, w_ffn2_gate,
                 w_ffn2_up, w_ffn2_down, ln_final):
    row = lambda a: a[:, None, :]
    wgu1, wd1 = _prep_ffn(w_ffn1_gate, w_ffn1_up, w_ffn1_down)
    wgu2, wd2 = _prep_ffn(w_ffn2_gate, w_ffn2_up, w_ffn2_down)
    return dict(
        ln_ffn1=row(ln_ffn1), wgu1=wgu1, wd1=wd1, ln_mix=row(ln_mix), w_in=w_in.astype(BF16),
        fold=_fold_fourier(w_fourier), conv_b_w=conv_b_w, conv_b_bias=row(conv_b_bias),
        ln_conv_gain=row(ln_conv_gain), ln_conv_bias=row(ln_conv_bias), conv_d_w=conv_d_w,
        w_out=w_out.astype(BF16), ln_ffn2=row(ln_ffn2), wgu2=wgu2, wd2=wd2,
        ln_final=ln_final[None, :])


def kernel(x_prompt, x_sample, ln_ffn1, w_ffn1_gate, w_ffn1_up, w_ffn1_down, ln_mix, w_in, w_fourier,
           conv_b_w, conv_b_bias, ln_conv_gain, ln_conv_bias, conv_d_w, w_out, ln_ffn2, w_ffn2_gate,
           w_ffn2_up, w_ffn2_down, ln_final):
    p = _prep_params(ln_ffn1, w_ffn1_gate, w_ffn1_up, w_ffn1_down, ln_mix, w_in, w_fourier, conv_b_w,
                     conv_b_bias, ln_conv_gain, ln_conv_bias, conv_d_w, w_out, ln_ffn2, w_ffn2_gate,
                     w_ffn2_up, w_ffn2_down, ln_final)
    n_layers = w_in.shape[0]
    return _trunk(x_prompt, p, n_layers), _trunk(x_sample, p, n_layers)
```

```python
import functools
import math

import jax
import jax.numpy as jnp
from jax import lax
from jax.experimental import pallas as pl
from jax.experimental.pallas import tpu as pltpu

F32 = jnp.float32
BF16 = jnp.bfloat16

N_FOURIER_GROUPS = 4
CONF_KERNEL = 31
SHORT_KERNEL = 3
HEAD_DIM = 64
DILATED_CONFIGS = ((128, 1), (512, 4), (2048, 16))
RMS_EPS = 1e-6
LN_EPS = 1e-5
NEG_BIG = -1e30

LANES = 128
SUBLANES_BF16 = 16
VMEM_LIMIT_BYTES = 60 * 1024 * 1024

FFN_TM = 1024
FFN_TF = 512
PROJ_TM = 512
DFT_TM = 1024
DFT_TK = 2048
DFT_BB = 2
DFT_GEN_ROWS = 128
CONV_TS = 512
CONV_HALO = 16
CONV_RC = 64
ATT_TQ = 256
ATT_SUB = 128
NORM_TM = 1024


def _params(*sem):
    return pltpu.CompilerParams(dimension_semantics=sem, vmem_limit_bytes=VMEM_LIMIT_BYTES)


def _rms(x, gain):
    y = x * lax.rsqrt(jnp.mean(x * x, axis=-1, keepdims=True) + RMS_EPS)
    return y * gain


def _sigmoid(x):
    return 1.0 / (1.0 + jnp.exp(-x))


def _ffn_body(layer_ref, x_ref, gain_ref, wgu_ref, wd_ref, o_ref, xn_ref, *, tf):
    j = pl.program_id(1)

    @pl.when(j == 0)
    def _():
        x = x_ref[...]
        xn_ref[...] = _rms(x, gain_ref[...]).astype(BF16)
        o_ref[...] = x

    gu = jnp.dot(xn_ref[...], wgu_ref[...], preferred_element_type=F32)
    g = gu[:, :tf]
    u = gu[:, tf:]
    h = (g * _sigmoid(g)) * u * 0.5
    o_ref[...] += jnp.dot(h.astype(BF16), wd_ref[...], preferred_element_type=F32)


def _ffn(layer, x, gain, wgu, wd):
    t, d = x.shape
    tf = FFN_TF
    nf = wd.shape[1] // tf
    tm = FFN_TM
    return pl.pallas_call(
        functools.partial(_ffn_body, tf=tf),
        grid_spec=pltpu.PrefetchScalarGridSpec(
            num_scalar_prefetch=1,
            grid=(t // tm, nf),
            in_specs=[
                pl.BlockSpec((tm, d), lambda i, j, l: (i, 0), pipeline_mode=pl.Buffered(1)),
                pl.BlockSpec((None, 1, d), lambda i, j, l: (l[0], 0, 0)),
                pl.BlockSpec((None, d, 2 * tf), lambda i, j, l: (l[0], 0, j)),
                pl.BlockSpec((None, tf, d), lambda i, j, l: (l[0], j, 0)),
            ],
            out_specs=pl.BlockSpec((tm, d), lambda i, j, l: (i, 0)),
            scratch_shapes=[pltpu.VMEM((tm, d), BF16)],
        ),
        out_shape=jax.ShapeDtypeStruct((t, d), F32),
        compiler_params=_params("parallel", "arbitrary"),
        name="ffn",
    )(layer, x, gain, wgu, wd)


def _fold_body(w_ref, o_ref):
    n = w_ref.shape[0]
    r = lax.broadcasted_iota(jnp.int32, (n, n), 0)
    c = lax.broadcasted_iota(jnp.int32, (n, n), 1)
    k = (r * c) & (n - 1)
    k = jnp.where(k >= n // 2, k - n, k)
    ang = k.astype(F32) * (2.0 * math.pi / n)
    w = w_ref[...]
    scale = n ** -0.5
    a = jnp.dot(jnp.cos(ang), w, preferred_element_type=F32, precision=lax.Precision.HIGHEST)
    b = jnp.dot(jnp.sin(ang), w, preferred_element_type=F32, precision=lax.Precision.HIGHEST)
    o_ref[:, :n] = (a * scale).astype(BF16)
    o_ref[:, n:] = (b * -scale).astype(BF16)


def _fold_fourier(w_fourier):
    nl, ng, c, _ = w_fourier.shape
    return pl.pallas_call(
        _fold_body,
        grid=(nl, ng),
        in_specs=[pl.BlockSpec((None, None, c, c), lambda l, g: (l, g, 0, 0))],
        out_specs=pl.BlockSpec((None, None, c, 2 * c), lambda l, g: (l, g, 0, 0)),
        out_shape=jax.ShapeDtypeStruct((nl, ng, c, 2 * c), BF16),
        compiler_params=_params("parallel", "parallel"),
        name="fold_fourier",
    )(w_fourier)


def _dft_gen_body(cos_ref, sin_ref, cb_ref, sb_ref, *, s, rows):
    i = pl.program_id(0)

    def angle(prod):
        k = prod & (s - 1)
        k = jnp.where(k >= s // 2, k - s, k)
        return k.astype(F32) * (2.0 * math.pi / s)

    @pl.when(i == 0)
    def _():
        r = lax.broadcasted_iota(jnp.int32, (rows, s), 0)
        c = lax.broadcasted_iota(jnp.int32, (rows, s), 1)
        b = angle(r * c)
        cb_ref[...] = jnp.cos(b)
        sb_ref[...] = jnp.sin(b)

    a = angle((i * rows) * lax.broadcasted_iota(jnp.int32, (1, s), 1))
    ca = jnp.cos(a)
    sa = jnp.sin(a)
    cb = cb_ref[...]
    sb = sb_ref[...]
    cos_ref[...] = (ca * cb - sa * sb).astype(BF16)
    sin_ref[...] = (sa * cb + ca * sb).astype(BF16)


def _dft_tables(s):
    assert s & (s - 1) == 0 and (s - 1) ** 2 < 2 ** 31
    rows = DFT_GEN_ROWS
    spec = pl.BlockSpec((rows, s), lambda i: (i, 0))
    return pl.pallas_call(
        functools.partial(_dft_gen_body, s=s, rows=rows),
        grid=(s // rows,),
        in_specs=[],
        out_specs=[spec, spec],
        out_shape=[jax.ShapeDtypeStruct((s, s), BF16)] * 2,
        scratch_shapes=[pltpu.VMEM((rows, s), F32)] * 2,
        compiler_params=_params("arbitrary"),
        name="dft_tables",
    )()


def _dft_body(c_ref, s_ref, v1_ref, v2_ref, o_ref, acc_ref, *, bb, scale):
    kk = pl.program_id(2)

    @pl.when(kk == 0)
    def _():
        acc_ref[...] = jnp.zeros_like(acc_ref)

    for b in range(bb):
        acc_ref[b] += (jnp.dot(c_ref[...], v1_ref[b], preferred_element_type=F32)
                       + jnp.dot(s_ref[...], v2_ref[b], preferred_element_type=F32))

    @pl.when(kk == pl.num_programs(2) - 1)
    def _():
        o_ref[...] = (acc_ref[...] * scale).astype(BF16)


def _dft(cos_t, sin_t, v1, v2):
    b, s, w = v1.shape
    bb, tm, tk = DFT_BB, min(DFT_TM, s), min(DFT_TK, s)
    tab = pl.BlockSpec((tm, tk), lambda n, i, k: (i, k))
    vin = pl.BlockSpec((bb, tk, w), lambda n, i, k: (n, k, 0))
    return pl.pallas_call(
        functools.partial(_dft_body, bb=bb, scale=s ** -0.5),
        grid=(b // bb, s // tm, s // tk),
        in_specs=[tab, tab, vin, vin],
        out_specs=pl.BlockSpec((bb, tm, w), lambda n, i, k: (n, i, 0)),
        out_shape=jax.ShapeDtypeStruct((b, s, w), BF16),
        scratch_shapes=[pltpu.VMEM((bb, tm, w), F32)],
        compiler_params=_params("parallel", "parallel", "arbitrary"),
        name="dft",
    )(cos_t, sin_t, v1, v2)


def _in_proj_body(layer_ref, x_ref, gain_ref, w_ref, fold_ref,
                  v1_ref, v2_ref, glu_ref, q_ref, k_ref, v_ref, sb_ref, sg_ref, *, w, ng):
    xn = _rms(x_ref[...], gain_ref[...]).astype(BF16)

    def proj(g):
        return jnp.dot(xn, w_ref[:, g * w:(g + 1) * w], preferred_element_type=F32)

    ua = proj(0).astype(BF16)
    c = w // ng
    for g in range(ng):
        vv = jnp.dot(ua[:, g * c:(g + 1) * c], fold_ref[g], preferred_element_type=F32)
        v1_ref[:, g * c:(g + 1) * c] = vv[:, :c].astype(BF16)
        v2_ref[:, g * c:(g + 1) * c] = vv[:, c:].astype(BF16)
    glu_ref[...] = proj(1) * _sigmoid(proj(2))
    q_ref[...] = (proj(3) * HEAD_DIM ** -0.5).astype(BF16)
    k_ref[...] = proj(4).astype(BF16)
    v_ref[...] = proj(5).astype(BF16)
    sb_ref[...] = proj(6)
    sg_ref[...] = proj(7) * proj(8)


def _in_proj(layer, x, gain, w_in, fold):
    t, d = x.shape
    ng, c = fold.shape[1], fold.shape[2]
    w = ng * c
    tm = PROJ_TM
    out = pl.BlockSpec((tm, w), lambda i, l: (i, 0))
    dts = (BF16, BF16, F32, BF16, BF16, BF16, F32, F32)
    return pl.pallas_call(
        functools.partial(_in_proj_body, w=w, ng=ng),
        grid_spec=pltpu.PrefetchScalarGridSpec(
            num_scalar_prefetch=1,
            grid=(t // tm,),
            in_specs=[
                pl.BlockSpec((tm, d), lambda i, l: (i, 0)),
                pl.BlockSpec((None, 1, d), lambda i, l: (l[0], 0, 0)),
                pl.BlockSpec((None, d, w_in.shape[2]), lambda i, l: (l[0], 0, 0),
                             pipeline_mode=pl.Buffered(1)),
                pl.BlockSpec((None, ng, c, 2 * c), lambda i, l: (l[0], 0, 0, 0)),
            ],
            out_specs=[out] * len(dts),
        ),
        out_shape=[jax.ShapeDtypeStruct((t, w), dt) for dt in dts],
        compiler_params=_params("parallel"),
        name="in_proj",
    )(layer, x, gain, w_in, fold)


def _conv_body(layer_ref, h_ref, hp_ref, hn_ref, g_ref, gp_ref, gn_ref, sb_ref,
               wb_ref, bias_ref, lng_ref, lnb_ref, wd_ref, yb_ref, yd_ref, hbuf, gbuf, *, ts):
    i = pl.program_id(1)
    first = i == 0
    last = i == pl.num_programs(1) - 1
    halo = CONV_HALO
    for buf, prev, cur, nxt in ((hbuf, hp_ref, h_ref, hn_ref), (gbuf, gp_ref, g_ref, gn_ref)):
        buf[0:halo] = jnp.where(first, 0.0, prev[...])
        buf[halo:halo + ts] = cur[...]
        buf[halo + ts:] = jnp.where(last, 0.0, nxt[...])

    kb, kd = CONF_KERNEL, SHORT_KERNEL
    for r0 in range(0, ts, CONV_RC):
        rows = slice(r0, r0 + CONV_RC)
        acc = jnp.zeros((CONV_RC, h_ref.shape[-1]), F32) + bias_ref[...]
        for k in range(kb):
            s0 = r0 + halo - kb // 2 + k
            acc = acc + wb_ref[k:k + 1, :] * hbuf[s0:s0 + CONV_RC, :]
        mu = jnp.mean(acc, axis=-1, keepdims=True)
        cen = acc - mu
        var = jnp.mean(cen * cen, axis=-1, keepdims=True)
        y = cen * lax.rsqrt(var + LN_EPS) * lng_ref[...] + lnb_ref[...]
        yb_ref[rows, :] = (y * _sigmoid(y)).astype(BF16)

        acd = jnp.zeros((CONV_RC, g_ref.shape[-1]), F32)
        for k in range(kd):
            s0 = r0 + halo - kd // 2 + k
            acd = acd + wd_ref[k:k + 1, :] * gbuf[s0:s0 + CONV_RC, :]
        yd_ref[rows, :] = (sb_ref[rows, :] * acd).astype(BF16)


def _conv(layer, glu, sg, sb, conv_b_w, conv_b_bias, ln_g, ln_b, conv_d_w):
    b, s, w = glu.shape
    ts, halo = CONV_TS, CONV_HALO
    nh = ts // halo
    main = pl.BlockSpec((None, ts, w), lambda n, i, l: (n, i, 0))
    prev = pl.BlockSpec((None, halo, w), lambda n, i, l: (n, jnp.maximum(i * nh - 1, 0), 0))
    nxt = pl.BlockSpec((None, halo, w), lambda n, i, l: (n, jnp.minimum((i + 1) * nh, s // halo - 1), 0))

    def per_layer(a):
        return pl.BlockSpec((None,) + a.shape[1:], lambda n, i, l: (l[0], 0, 0))

    return pl.pallas_call(
        functools.partial(_conv_body, ts=ts),
        grid_spec=pltpu.PrefetchScalarGridSpec(
            num_scalar_prefetch=1,
            grid=(b, s // ts),
            in_specs=[main, prev, nxt, main, prev, nxt, main,
                      per_layer(conv_b_w), per_layer(conv_b_bias), per_layer(ln_g), per_layer(ln_b),
                      per_layer(conv_d_w)],
            out_specs=[main, main],
            scratch_shapes=[pltpu.VMEM((ts + 2 * halo, w), F32)] * 2,
        ),
        out_shape=[jax.ShapeDtypeStruct((b, s, w), BF16)] * 2,
        compiler_params=_params("parallel", "parallel"),
        name="conv",
    )(layer, glu, glu, glu, sg, sg, sg, sb, conv_b_w, conv_b_bias, ln_g, ln_b, conv_d_w)


def _band_attn_body(q_ref, kp_ref, k_ref, kn_ref, vp_ref, v_ref, vn_ref, acc_ref, st_ref, kbuf, vbuf,
                    *, tq, sub, half, dil, nl, nh):
    i = pl.program_id(2)
    for buf, prev, cur, nxt in ((kbuf, kp_ref, k_ref, kn_ref), (vbuf, vp_ref, v_ref, vn_ref)):
        buf[0:half] = prev[...]
        buf[half:half + tq] = cur[...]
        buf[half + tq:] = nxt[...]

    win = sub + 2 * half
    e = HEAD_DIM
    lane = lax.broadcasted_iota(jnp.int32, (sub, 2 * e), 1)
    st_lane = lax.broadcasted_iota(jnp.int32, (sub, st_ref.shape[-1]), 1)
    for so in range(0, tq, sub):
        r = lax.broadcasted_iota(jnp.int32, (sub, win), 0)
        c = lax.broadcasted_iota(jnp.int32, (sub, win), 1)
        ad = jnp.abs(r + half - c)
        key_row = i * tq + (so - half) + c
        valid = jnp.logical_and(ad <= half, jnp.logical_and(key_row >= 0, key_row < nl))
        dist = (ad * dil).astype(F32)
        stats = jnp.zeros(st_lane.shape, F32)
        for hp in range(nh // 2):
            lanes = slice(hp * 2 * e, (hp + 1) * 2 * e)
            qp = q_ref[so:so + sub, lanes]
            kw = kbuf[so:so + win, lanes]
            vw = vbuf[so:so + win, lanes]
            res = None
            for hh in range(2):
                h = 2 * hp + hh
                mine = (lane < e) if hh == 0 else (lane >= e)
                qh = jnp.where(mine, qp, jnp.zeros_like(qp))
                s = lax.dot_general(qh, kw, (((1,), (1,)), ((), ())), preferred_element_type=F32)
                slope = 2.0 ** (-8.0 * (h + 1) / nh)
                s = jnp.where(valid, s - slope * dist, NEG_BIG)
                m = jnp.max(s, axis=-1, keepdims=True)
                p = jnp.exp(s - m)
                den = jnp.sum(p, axis=-1, keepdims=True)
                pv = jnp.dot(p.astype(BF16), vw, preferred_element_type=F32)
                res = pv if res is None else jnp.where(mine, pv, res)
                stats = jnp.where(st_lane == h, m, stats)
                stats = jnp.where(st_lane == nh + h, den, stats)
            acc_ref[so:so + sub, lanes] = res
        st_ref[so:so + sub, :] = stats


def _band_attn(q, k, v, window, dil):
    b, s, w = q.shape
    half = window // (2 * dil)
    nl = s // dil
    assert s % dil == 0 and nl % half == 0 and half % SUBLANES_BF16 == 0
    tq = min(ATT_TQ, nl)
    sub = min(ATT_SUB, tq)
    nh = w // HEAD_DIM
    hb = tq // half
    view = lambda a: a.reshape(b, nl, dil * w)
    main = pl.BlockSpec((None, tq, w), lambda n, r, i: (n, i, r))
    prev = pl.BlockSpec((None, half, w), lambda n, r, i: (n, jnp.maximum(i * hb - 1, 0), r))
    nxt = pl.BlockSpec((None, half, w), lambda n, r, i: (n, jnp.minimum((i + 1) * hb, nl // half - 1), r))
    acc, st = pl.pallas_call(
        functools.partial(_band_attn_body, tq=tq, sub=sub, half=half, dil=dil, nl=nl, nh=nh),
        grid=(b, dil, nl // tq),
        in_specs=[main, prev, main, nxt, prev, main, nxt],
        out_specs=[main, pl.BlockSpec((None, tq, LANES), lambda n, r, i: (n, i, r))],
        out_shape=[jax.ShapeDtypeStruct((b, nl, dil * w), F32),
                   jax.ShapeDtypeStruct((b, nl, dil * LANES), F32)],
        scratch_shapes=[pltpu.VMEM((tq + 2 * half, w), BF16)] * 2,
        compiler_params=_params("parallel", "parallel", "parallel"),
        name="band_attn",
    )(view(q), view(k), view(k), view(k), view(v), view(v), view(v))
    return acc.reshape(b * s, w), st.reshape(b * s, LANES)


def _attn_merge_body(*refs, nh):
    nb = (len(refs) - 1) // 2
    accs, sts, o_ref = refs[:nb], refs[nb:2 * nb], refs[-1]
    e = HEAD_DIM
    for h in range(nh):
        cols = slice(h * e, (h + 1) * e)
        ms = [st[:, h:h + 1] for st in sts]
        dens = [st[:, nh + h:nh + h + 1] for st in sts]
        m_all = functools.reduce(jnp.maximum, ms)
        ws = [jnp.exp(m - m_all) for m in ms]
        num = sum(wt * acc[:, cols] for wt, acc in zip(ws, accs))
        den = sum(wt * dn for wt, dn in zip(ws, dens))
        o_ref[:, cols] = (num / den).astype(BF16)


def _attn(q, k, v):
    b, s, w = q.shape
    parts = [_band_attn(q, k, v, window, dil) for window, dil in DILATED_CONFIGS]
    accs = [pt[0] for pt in parts]
    sts = [pt[1] for pt in parts]
    t = b * s
    tm = PROJ_TM
    a_spec = pl.BlockSpec((tm, w), lambda i: (i, 0))
    s_spec = pl.BlockSpec((tm, LANES), lambda i: (i, 0))
    return pl.pallas_call(
        functools.partial(_attn_merge_body, nh=w // HEAD_DIM),
        grid=(t // tm,),
        in_specs=[a_spec] * len(accs) + [s_spec] * len(sts),
        out_specs=a_spec,
        out_shape=jax.ShapeDtypeStruct((t, w), BF16),
        compiler_params=_params("parallel"),
        name="attn_merge",
    )(*accs, *sts)


def _out_proj_body(layer_ref, x_ref, ya_ref, yb_ref, yc_ref, yd_ref, w_ref, o_ref, *, w):
    acc = x_ref[...]
    for g, y_ref in enumerate((ya_ref, yb_ref, yc_ref, yd_ref)):
        acc = acc + jnp.dot(y_ref[...], w_ref[g * w:(g + 1) * w, :], preferred_element_type=F32)
    o_ref[...] = acc


def _out_proj(layer, x, ya, yb, yc, yd, w_out):
    t, d = x.shape
    w = ya.shape[1]
    tm = PROJ_TM
    xs = pl.BlockSpec((tm, d), lambda i, l: (i, 0))
    ys = pl.BlockSpec((tm, w), lambda i, l: (i, 0))
    return pl.pallas_call(
        functools.partial(_out_proj_body, w=w),
        grid_spec=pltpu.PrefetchScalarGridSpec(
            num_scalar_prefetch=1,
            grid=(t // tm,),
            in_specs=[xs, ys, ys, ys, ys,
                      pl.BlockSpec((None,) + w_out.shape[1:], lambda i, l: (l[0], 0, 0),
                                   pipeline_mode=pl.Buffered(1))],
            out_specs=xs,
        ),
        out_shape=jax.ShapeDtypeStruct((t, d), F32),
        compiler_params=_params("parallel"),
        name="out_proj",
    )(layer, x, ya, yb, yc, yd, w_out)


def _norm_body(x_ref, gain_ref, o_ref):
    o_ref[...] = _rms(x_ref[...], gain_ref[...])


def _final_norm(x, gain):
    t, d = x.shape
    tm = NORM_TM
    xs = pl.BlockSpec((tm, d), lambda i: (i, 0))
    return pl.pallas_call(
        _norm_body,
        grid=(t // tm,),
        in_specs=[xs, pl.BlockSpec((1, d), lambda i: (0, 0))],
        out_specs=xs,
        out_shape=jax.ShapeDtypeStruct((t, d), F32),
        compiler_params=_params("parallel"),
        name="final_norm",
    )(x, gain)


def _prep_ffn(wg, wu, wd):
    nl, d, f = wg.shape
    tf = FFN_TF
    nf = -(-f // tf)
    pad = nf * tf - f
    wg = jnp.pad(wg.astype(BF16), ((0, 0), (0, 0), (0, pad))).reshape(nl, d, nf, tf)
    wu = jnp.pad(wu.astype(BF16), ((0, 0), (0, 0), (0, pad))).reshape(nl, d, nf, tf)
    wgu = jnp.concatenate([wg, wu], axis=3).reshape(nl, d, nf * 2 * tf)
    wd = jnp.pad(wd.astype(BF16), ((0, 0), (0, pad), (0, 0)))
    return wgu, wd


def _trunk(x, p, n_layers):
    b, s, d = x.shape
    cos_t, sin_t = _dft_tables(s)

    def seq(a):
        return a.reshape(b, s, a.shape[-1])

    def layer_fn(li, x2):
        layer = jnp.full((1,), li, jnp.int32)
        x2 = _ffn(layer, x2, p["ln_ffn1"], p["wgu1"], p["wd1"])
        v1, v2, glu, q, k, v, sb, sg = _in_proj(layer, x2, p["ln_mix"], p["w_in"], p["fold"])
        ya = _dft(cos_t, sin_t, seq(v1), seq(v2))
        yb, yd = _conv(layer, seq(glu), seq(sg), seq(sb), p["conv_b_w"], p["conv_b_bias"],
                       p["ln_conv_gain"], p["ln_conv_bias"], p["conv_d_w"])
        yc = _attn(seq(q), seq(k), seq(v))
        flat = lambda a: a.reshape(b * s, a.shape[-1])
        x2 = _out_proj(layer, x2, flat(ya), flat(yb), flat(yc), flat(yd), p["w_out"])
        return _ffn(layer, x2, p["ln_ffn2"], p["wgu2"], p["wd2"])

    x2 = lax.fori_loop(0, n_layers, layer_fn, x.reshape(b * s, d))
    return _final_norm(x2, p["ln_final"]).reshape(b, s, d)


def _prep_params(ln_ffn1, w_ffn1_gate, w_ffn1_up, w_ffn1_down, ln_mix, w_in, w_fourier, conv_b_w,
                 conv_b_bias, ln_conv_gain, ln_conv_bias, conv_d_w, w_out, ln_ffn2, w_ffn2_gate,
                 w_ffn2_up, w_ffn2_down, ln_final):
    row = lambda a: a[:, None, :]
    wgu1, wd1 = _prep_ffn(w_ffn1_gate, w_ffn1_up, w_ffn1_down)
    wgu2, wd2 = _prep_ffn(w_ffn2_gate, w_ffn2_up, w_ffn2_down)
    return dict(
        ln_ffn1=row(ln_ffn1), wgu1=wgu1, wd1=wd1, ln_mix=row(ln_mix), w_in=w_in.astype(BF16),
        fold=_fold_fourier(w_fourier), conv_b_w=conv_b_w, conv_b_bias=row(conv_b_bias),
        ln_conv_gain=row(ln_conv_gain), ln_conv_bias=row(ln_conv_bias), conv_d_w=conv_d_w,
        w_out=w_out.astype(BF16), ln_ffn2=row(ln_ffn2), wgu2=wgu2, wd2=wd2,
        ln_final=ln_final[None, :])


def kernel(x_prompt, x_sample, ln_ffn1, w_ffn1_gate, w_ffn1_up, w_ffn1_down, ln_mix, w_in, w_fourier,
           conv_b_w, conv_b_bias, ln_conv_gain, ln_conv_bias, conv_d_w, w_out, ln_ffn2, w_ffn2_gate,
           w_ffn2_up, w_ffn2_down, ln_final):
    p = _prep_params(ln_ffn1, w_ffn1_gate, w_ffn1_up, w_ffn1_down, ln_mix, w_in, w_fourier, conv_b_w,
                     conv_b_bias, ln_conv_gain, ln_conv_bias, conv_d_w, w_out, ln_ffn2, w_ffn2_gate,
                     w_ffn2_up, w_ffn2_down, ln_final)
    n_layers = w_in.shape[0]
    return _trunk(x_prompt, p, n_layers), _trunk(x_sample, p, n_layers)
```

```python
import functools
import math

import jax
import jax.numpy as jnp
from jax import lax
from jax.experimental import pallas as pl
from jax.experimental.pallas import tpu as pltpu

F32 = jnp.float32
BF16 = jnp.bfloat16

N_FOURIER_GROUPS = 4
CONF_KERNEL = 31
SHORT_KERNEL = 3
HEAD_DIM = 64
DILATED_CONFIGS = ((128, 1), (512, 4), (2048, 16))
RMS_EPS = 1e-6
LN_EPS = 1e-5
NEG_BIG = -1e30

LANES = 128
SUBLANES_F32 = 8
SUBLANES_BF16 = 16
VMEM_LIMIT_BYTES = 60 * 1024 * 1024

FFN_TM = 1024
FFN_TF = 512
PROJ_TM = 512
DFT_TM = 1024
DFT_TK = 2048
DFT_BB = 2
DFT_GEN_ROWS = 128
CONV_TS = 512
CONV_HALO = 16
CONV_RC = 64
ATT_TQ = 256
ATT_SUB = 128
NORM_TM = 1024


def _params(*sem):
    return pltpu.CompilerParams(dimension_semantics=sem, vmem_limit_bytes=VMEM_LIMIT_BYTES)


def _rms(x, gain):
    y = x * lax.rsqrt(jnp.mean(x * x, axis=-1, keepdims=True) + RMS_EPS)
    return y * gain


def _sigmoid(x):
    return 1.0 / (1.0 + jnp.exp(-x))


def _ffn_body(layer_ref, x_ref, gain_ref, wgu_ref, wd_ref, o_ref, xn_ref, *, tf):
    j = pl.program_id(1)

    @pl.when(j == 0)
    def _():
        x = x_ref[...]
        xn_ref[...] = _rms(x, gain_ref[...]).astype(BF16)
        o_ref[...] = x

    gu = jnp.dot(xn_ref[...], wgu_ref[...], preferred_element_type=F32)
    g = gu[:, :tf]
    u = gu[:, tf:]
    h = (g * _sigmoid(g)) * u * 0.5
    o_ref[...] += jnp.dot(h.astype(BF16), wd_ref[...], preferred_element_type=F32)


def _ffn(layer, x, gain, wgu, wd):
    t, d = x.shape
    tf = FFN_TF
    nf = wd.shape[1] // tf
    tm = FFN_TM
    return pl.pallas_call(
        functools.partial(_ffn_body, tf=tf),
        grid_spec=pltpu.PrefetchScalarGridSpec(
            num_scalar_prefetch=1,
            grid=(t // tm, nf),
            in_specs=[
                pl.BlockSpec((tm, d), lambda i, j, l: (i, 0)),
                pl.BlockSpec((None, 1, d), lambda i, j, l: (l[0], 0, 0)),
                pl.BlockSpec((None, d, 2 * tf), lambda i, j, l: (l[0], 0, j)),
                pl.BlockSpec((None, tf, d), lambda i, j, l: (l[0], j, 0)),
            ],
            out_specs=pl.BlockSpec((tm, d), lambda i, j, l: (i, 0)),
            scratch_shapes=[pltpu.VMEM((tm, d), BF16)],
        ),
        out_shape=jax.ShapeDtypeStruct((t, d), F32),
        compiler_params=_params("parallel", "arbitrary"),
        name="ffn",
    )(layer, x, gain, wgu, wd)


def _fold_body(w_ref, o_ref):
    n = w_ref.shape[0]
    r = lax.broadcasted_iota(jnp.int32, (n, n), 0)
    c = lax.broadcasted_iota(jnp.int32, (n, n), 1)
    k = (r * c) & (n - 1)
    k = jnp.where(k >= n // 2, k - n, k)
    ang = k.astype(F32) * (2.0 * math.pi / n)
    w = w_ref[...]
    scale = n ** -0.5
    a = jnp.dot(jnp.cos(ang), w, preferred_element_type=F32, precision=lax.Precision.HIGHEST)
    b = jnp.dot(jnp.sin(ang), w, preferred_element_type=F32, precision=lax.Precision.HIGHEST)
    o_ref[:, :n] = (a * scale).astype(BF16)
    o_ref[:, n:] = (b * -scale).astype(BF16)


def _fold_fourier(w_fourier):
    nl, ng, c, _ = w_fourier.shape
    return pl.pallas_call(
        _fold_body,
        grid=(nl, ng),
        in_specs=[pl.BlockSpec((None, None, c, c), lambda l, g: (l, g, 0, 0))],
        out_specs=pl.BlockSpec((None, None, c, 2 * c), lambda l, g: (l, g, 0, 0)),
        out_shape=jax.ShapeDtypeStruct((nl, ng, c, 2 * c), BF16),
        compiler_params=_params("parallel", "parallel"),
        name="fold_fourier",
    )(w_fourier)


def _dft_gen_body(cos_ref, sin_ref, cb_ref, sb_ref, *, s, rows):
    i = pl.program_id(0)

    def angle(prod):
        k = prod & (s - 1)
        k = jnp.where(k >= s // 2, k - s, k)
        return k.astype(F32) * (2.0 * math.pi / s)

    @pl.when(i == 0)
    def _():
        r = lax.broadcasted_iota(jnp.int32, (rows, s), 0)
        c = lax.broadcasted_iota(jnp.int32, (rows, s), 1)
        b = angle(r * c)
        cb_ref[...] = jnp.cos(b)
        sb_ref[...] = jnp.sin(b)

    a = angle((i * rows) * lax.broadcasted_iota(jnp.int32, (1, s), 1))
    ca = jnp.cos(a)
    sa = jnp.sin(a)
    cb = cb_ref[...]
    sb = sb_ref[...]
    cos_ref[...] = (ca * cb - sa * sb).astype(BF16)
    sin_ref[...] = (sa * cb + ca * sb).astype(BF16)


def _dft_tables(s):
    assert s & (s - 1) == 0 and (s - 1) ** 2 < 2 ** 31
    rows = DFT_GEN_ROWS
    spec = pl.BlockSpec((rows, s), lambda i: (i, 0))
    return pl.pallas_call(
        functools.partial(_dft_gen_body, s=s, rows=rows),
        grid=(s // rows,),
        in_specs=[],
        out_specs=[spec, spec],
        out_shape=[jax.ShapeDtypeStruct((s, s), BF16)] * 2,
        scratch_shapes=[pltpu.VMEM((rows, s), F32)] * 2,
        compiler_params=_params("arbitrary"),
        name="dft_tables",
    )()


def _dft_body(c_ref, s_ref, v1_ref, v2_ref, o_ref, acc_ref, *, bb, scale):
    kk = pl.program_id(2)

    @pl.when(kk == 0)
    def _():
        acc_ref[...] = jnp.zeros_like(acc_ref)

    for b in range(bb):
        acc_ref[b] += (jnp.dot(c_ref[...], v1_ref[b], preferred_element_type=F32)
                       + jnp.dot(s_ref[...], v2_ref[b], preferred_element_type=F32))

    @pl.when(kk == pl.num_programs(2) - 1)
    def _():
        o_ref[...] = (acc_ref[...] * scale).astype(BF16)


def _dft(cos_t, sin_t, v1, v2):
    b, s, w = v1.shape
    bb, tm, tk = DFT_BB, min(DFT_TM, s), min(DFT_TK, s)
    tab = pl.BlockSpec((tm, tk), lambda n, i, k: (i, k))
    vin = pl.BlockSpec((bb, tk, w), lambda n, i, k: (n, k, 0))
    return pl.pallas_call(
        functools.partial(_dft_body, bb=bb, scale=s ** -0.5),
        grid=(b // bb, s // tm, s // tk),
        in_specs=[tab, tab, vin, vin],
        out_specs=pl.BlockSpec((bb, tm, w), lambda n, i, k: (n, i, 0)),
        out_shape=jax.ShapeDtypeStruct((b, s, w), BF16),
        scratch_shapes=[pltpu.VMEM((bb, tm, w), F32)],
        compiler_params=_params("parallel", "parallel", "arbitrary"),
        name="dft",
    )(cos_t, sin_t, v1, v2)


def _in_proj_body(layer_ref, x_ref, gain_ref, w_ref, fold_ref, v1_ref, v2_ref, glu_ref, sb_ref, sg_ref,
                  *rest, w, ng, dils):
    qkv_refs, pbuf = rest[:-1], rest[-1]
    xn = _rms(x_ref[...], gain_ref[...]).astype(BF16)

    def proj(g):
        return jnp.dot(xn, w_ref[:, g * w:(g + 1) * w], preferred_element_type=F32)

    ua = proj(0).astype(BF16)
    c = w // ng
    for g in range(ng):
        vv = jnp.dot(ua[:, g * c:(g + 1) * c], fold_ref[g], preferred_element_type=F32)
        v1_ref[:, g * c:(g + 1) * c] = vv[:, :c].astype(BF16)
        v2_ref[:, g * c:(g + 1) * c] = vv[:, c:].astype(BF16)
    glu_ref[...] = proj(1) * _sigmoid(proj(2))
    tm = x_ref.shape[0]
    for n, scale in enumerate((HEAD_DIM ** -0.5, 1.0, 1.0)):
        val = proj(3 + n) * scale
        for c in range(w // LANES):
            pbuf[n, c] = val[:, c * LANES:(c + 1) * LANES]
        for m, dil in enumerate(dils):
            out = qkv_refs[n * len(dils) + m]
            if dil == 1:
                out[...] = val.astype(BF16)
                continue
            for r in range(dil):
                for c in range(w // LANES):
                    rows = pbuf[n, c, pl.ds(r, tm // dil, stride=dil), :]
                    out[:, r * w + c * LANES:r * w + (c + 1) * LANES] = rows.astype(BF16)
    sb_ref[...] = proj(6)
    sg_ref[...] = proj(7) * proj(8)


def _in_proj(layer, x, gain, w_in, fold):
    t, d = x.shape
    ng, c = fold.shape[1], fold.shape[2]
    w = ng * c
    tm = PROJ_TM
    dils = tuple(dil for _, dil in DILATED_CONFIGS)
    assert all(tm % (dil * SUBLANES_BF16) == 0 for dil in dils)
    out = pl.BlockSpec((tm, w), lambda i, l: (i, 0))
    dts = (BF16, BF16, F32, F32, F32)
    views = [(tm // dil, dil * w) for _ in range(3) for dil in dils]
    return pl.pallas_call(
        functools.partial(_in_proj_body, w=w, ng=ng, dils=dils),
        grid_spec=pltpu.PrefetchScalarGridSpec(
            num_scalar_prefetch=1,
            grid=(t // tm,),
            in_specs=[
                pl.BlockSpec((tm, d), lambda i, l: (i, 0)),
                pl.BlockSpec((None, 1, d), lambda i, l: (l[0], 0, 0)),
                pl.BlockSpec((None, d, w_in.shape[2]), lambda i, l: (l[0], 0, 0),
                             pipeline_mode=pl.Buffered(1)),
                pl.BlockSpec((None, ng, c, 2 * c), lambda i, l: (l[0], 0, 0, 0)),
            ],
            out_specs=[out] * len(dts) + [pl.BlockSpec(blk, lambda i, l: (i, 0)) for blk in views],
            scratch_shapes=[pltpu.VMEM((3, w // LANES, tm, LANES), F32)],
        ),
        out_shape=([jax.ShapeDtypeStruct((t, w), dt) for dt in dts]
                   + [jax.ShapeDtypeStruct((t // tm * blk[0], blk[1]), BF16) for blk in views]),
        compiler_params=_params("parallel"),
        name="in_proj",
    )(layer, x, gain, w_in, fold)


def _conv_body(layer_ref, h_ref, hp_ref, hn_ref, g_ref, gp_ref, gn_ref, sb_ref,
               wb_ref, bias_ref, lng_ref, lnb_ref, wd_ref, yb_ref, yd_ref, hbuf, gbuf, hsh, gsh, *, ts):
    i = pl.program_id(1)
    first = i == 0
    last = i == pl.num_programs(1) - 1
    halo = CONV_HALO
    for buf, prev, cur, nxt in ((hbuf, hp_ref, h_ref, hn_ref), (gbuf, gp_ref, g_ref, gn_ref)):
        buf[0:halo] = jnp.where(first, 0.0, prev[...])
        buf[halo:halo + ts] = cur[...]
        buf[halo + ts:] = jnp.where(last, 0.0, nxt[...])

    kb, kd = CONF_KERNEL, SHORT_KERNEL
    span = ts + 2 * halo - SUBLANES_F32
    h_shifts = tuple(range(1, SUBLANES_F32))
    g_shifts = tuple(sorted({(halo - kd // 2 + k) % SUBLANES_F32 for k in range(kd)} - {0}))
    for buf, shifted, shifts in ((hbuf, hsh, h_shifts), (gbuf, gsh, g_shifts)):
        for j, sh in enumerate(shifts):
            shifted[j, 0:span] = buf[sh:sh + span]

    def window(buf, shifted, shifts, start):
        sh = start % SUBLANES_F32
        if sh == 0:
            return buf[start:start + CONV_RC, :]
        return shifted[shifts.index(sh), start - sh:start - sh + CONV_RC, :]

    for r0 in range(0, ts, CONV_RC):
        rows = slice(r0, r0 + CONV_RC)
        acc = jnp.zeros((CONV_RC, h_ref.shape[-1]), F32) + bias_ref[...]
        for k in range(kb):
            acc = acc + wb_ref[k:k + 1, :] * window(hbuf, hsh, h_shifts, r0 + halo - kb // 2 + k)
        mu = jnp.mean(acc, axis=-1, keepdims=True)
        cen = acc - mu
        var = jnp.mean(cen * cen, axis=-1, keepdims=True)
        y = cen * lax.rsqrt(var + LN_EPS) * lng_ref[...] + lnb_ref[...]
        yb_ref[rows, :] = (y * _sigmoid(y)).astype(BF16)

        acd = jnp.zeros((CONV_RC, g_ref.shape[-1]), F32)
        for k in range(kd):
            acd = acd + wd_ref[k:k + 1, :] * window(gbuf, gsh, g_shifts, r0 + halo - kd // 2 + k)
        yd_ref[rows, :] = (sb_ref[rows, :] * acd).astype(BF16)


def _conv(layer, glu, sg, sb, conv_b_w, conv_b_bias, ln_g, ln_b, conv_d_w):
    b, s, w = glu.shape
    ts, halo = CONV_TS, CONV_HALO
    nh = ts // halo
    main = pl.BlockSpec((None, ts, w), lambda n, i, l: (n, i, 0))
    prev = pl.BlockSpec((None, halo, w), lambda n, i, l: (n, jnp.maximum(i * nh - 1, 0), 0))
    nxt = pl.BlockSpec((None, halo, w), lambda n, i, l: (n, jnp.minimum((i + 1) * nh, s // halo - 1), 0))

    def per_layer(a):
        return pl.BlockSpec((None,) + a.shape[1:], lambda n, i, l: (l[0], 0, 0))

    return pl.pallas_call(
        functools.partial(_conv_body, ts=ts),
        grid_spec=pltpu.PrefetchScalarGridSpec(
            num_scalar_prefetch=1,
            grid=(b, s // ts),
            in_specs=[main, prev, nxt, main, prev, nxt, main,
                      per_layer(conv_b_w), per_layer(conv_b_bias), per_layer(ln_g), per_layer(ln_b),
                      per_layer(conv_d_w)],
            out_specs=[main, main],
            scratch_shapes=[pltpu.VMEM((ts + 2 * halo, w), F32)] * 2 + [
                pltpu.VMEM((SUBLANES_F32 - 1, ts + 2 * halo, w), F32),
                pltpu.VMEM((min(SHORT_KERNEL, SUBLANES_F32) - 1, ts + 2 * halo, w), F32)],
        ),
        out_shape=[jax.ShapeDtypeStruct((b, s, w), BF16)] * 2,
        compiler_params=_params("parallel", "parallel"),
        name="conv",
    )(layer, glu, glu, glu, sg, sg, sg, sb, conv_b_w, conv_b_bias, ln_g, ln_b, conv_d_w)


def _band_attn_body(q_ref, kp_ref, k_ref, kn_ref, vp_ref, v_ref, vn_ref, acc_ref, st_ref, kbuf, vbuf,
                    *, tq, sub, half, dil, nl, nh):
    i = pl.program_id(1)
    res_class = pl.program_id(2)
    for buf, prev, cur, nxt in ((kbuf, kp_ref, k_ref, kn_ref), (vbuf, vp_ref, v_ref, vn_ref)):
        buf[0:half] = prev[...]
        buf[half:half + tq] = cur[...]
        buf[half + tq:] = nxt[...]

    win = sub + 2 * half
    e = HEAD_DIM
    lane = lax.broadcasted_iota(jnp.int32, (sub, 2 * e), 1)
    st_lane = lax.broadcasted_iota(jnp.int32, (sub, st_ref.shape[-1]), 1)
    for so in range(0, tq, sub):
        r = lax.broadcasted_iota(jnp.int32, (sub, win), 0)
        c = lax.broadcasted_iota(jnp.int32, (sub, win), 1)
        ad = jnp.abs(r + half - c)
        key_row = i * tq + (so - half) + c
        valid = jnp.logical_and(ad <= half, jnp.logical_and(key_row >= 0, key_row < nl))
        dist = (ad * dil).astype(F32)
        stats = jnp.zeros(st_lane.shape, F32)
        for hp in range(nh // 2):
            lanes = slice(hp * 2 * e, (hp + 1) * 2 * e)
            qp = q_ref[so:so + sub, lanes]
            kw = kbuf[so:so + win, lanes]
            vw = vbuf[so:so + win, lanes]
            res = None
            for hh in range(2):
                h = 2 * hp + hh
                mine = (lane < e) if hh == 0 else (lane >= e)
                qh = jnp.where(mine, qp, jnp.zeros_like(qp))
                s = lax.dot_general(qh, kw, (((1,), (1,)), ((), ())), preferred_element_type=F32)
                slope = 2.0 ** (-8.0 * (h + 1) / nh)
                s = jnp.where(valid, s - slope * dist, NEG_BIG)
                m = jnp.max(s, axis=-1, keepdims=True)
                p = jnp.exp(s - m)
                den = jnp.sum(p, axis=-1, keepdims=True)
                pv = jnp.dot(p.astype(BF16), vw, preferred_element_type=F32)
                res = pv if res is None else jnp.where(mine, pv, res)
                stats = jnp.where(st_lane == h, m, stats)
                stats = jnp.where(st_lane == nh + h, den, stats)
            if dil == 1:
                acc_ref[hp, so:so + sub, :] = res
            else:
                acc_ref[hp, pl.ds(so * dil + res_class, sub, stride=dil), :] = res
        if dil == 1:
            st_ref[so:so + sub, :] = stats
        else:
            st_ref[pl.ds(so * dil + res_class, sub, stride=dil), :] = stats


def _band_attn(q, k, v, b, s, window, dil):
    w = q.shape[1] // dil
    half = window // (2 * dil)
    nl = s // dil
    assert s % dil == 0 and nl % half == 0 and half % SUBLANES_BF16 == 0
    tq = min(ATT_TQ, nl)
    sub = min(ATT_SUB, tq)
    nh = w // HEAD_DIM
    hb = tq // half
    view = lambda a: a.reshape(b, nl, dil * w)
    main = pl.BlockSpec((None, tq, w), lambda n, i, r: (n, i, r))
    prev = pl.BlockSpec((None, half, w), lambda n, i, r: (n, jnp.maximum(i * hb - 1, 0), r))
    nxt = pl.BlockSpec((None, half, w), lambda n, i, r: (n, jnp.minimum((i + 1) * hb, nl // half - 1), r))
    acc, st = pl.pallas_call(
        functools.partial(_band_attn_body, tq=tq, sub=sub, half=half, dil=dil, nl=nl, nh=nh),
        grid=(b, nl // tq, dil),
        in_specs=[main, prev, main, nxt, prev, main, nxt],
        out_specs=[pl.BlockSpec((None, nh // 2, tq * dil, 2 * HEAD_DIM), lambda n, i, r: (n, 0, i, 0)),
                   pl.BlockSpec((None, tq * dil, LANES), lambda n, i, r: (n, i, 0))],
        out_shape=[jax.ShapeDtypeStruct((b, nh // 2, s, 2 * HEAD_DIM), F32),
                   jax.ShapeDtypeStruct((b, s, LANES), F32)],
        scratch_shapes=[pltpu.VMEM((tq + 2 * half, w), BF16)] * 2,
        compiler_params=_params("parallel", "parallel", "arbitrary"),
        name="band_attn",
    )(view(q), view(k), view(k), view(k), view(v), view(v), view(v))
    return acc, st


def _attn_merge_body(*refs, nh):
    nb = (len(refs) - 1) // 2
    accs, st_refs, o_ref = refs[:nb], refs[nb:2 * nb], refs[-1]
    sts = [st[...] for st in st_refs]
    nlane = sts[0].shape[-1]
    lane = lax.broadcasted_iota(jnp.int32, sts[0].shape, 1)
    m_all = functools.reduce(jnp.maximum, sts)
    ws = [jnp.exp(st - m_all) for st in sts]
    dens = [pltpu.roll(st, nlane - nh, 1) for st in sts]
    total = sum(wt * dn for wt, dn in zip(ws, dens))
    w_out = o_ref.shape[-1]
    spread = (lax.broadcasted_iota(jnp.int32, (nlane, w_out), 0)
              == lax.broadcasted_iota(jnp.int32, (nlane, w_out), 1) // HEAD_DIM).astype(F32)
    wide = [jnp.dot(jnp.where(lane < nh, wt / total, 0.0), spread, preferred_element_type=F32,
                    precision=lax.Precision.HIGHEST) for wt in ws]
    pw = accs[0].shape[-1]
    for hp in range(accs[0].shape[0]):
        cols = slice(hp * pw, (hp + 1) * pw)
        out = sum(acc[hp] * wd[:, cols] for acc, wd in zip(accs, wide))
        o_ref[:, cols] = out.astype(BF16)


def _attn(qkv, b, s):
    nd = len(DILATED_CONFIGS)
    parts = [_band_attn(qkv[m], qkv[nd + m], qkv[2 * nd + m], b, s, window, dil)
             for m, (window, dil) in enumerate(DILATED_CONFIGS)]
    accs = [pt[0] for pt in parts]
    sts = [pt[1] for pt in parts]
    _, npair, _, pw = accs[0].shape
    w = npair * pw
    tm = PROJ_TM
    a_spec = pl.BlockSpec((None, npair, tm, pw), lambda n, i: (n, 0, i, 0))
    s_spec = pl.BlockSpec((None, tm, LANES), lambda n, i: (n, i, 0))
    out = pl.pallas_call(
        functools.partial(_attn_merge_body, nh=w // HEAD_DIM),
        grid=(b, s // tm),
        in_specs=[a_spec] * len(accs) + [s_spec] * len(sts),
        out_specs=pl.BlockSpec((None, tm, w), lambda n, i: (n, i, 0)),
        out_shape=jax.ShapeDtypeStruct((b, s, w), BF16),
        compiler_params=_params("parallel", "parallel"),
        name="attn_merge",
    )(*accs, *sts)
    return out.reshape(b * s, w)


def _out_proj_body(layer_ref, x_ref, ya_ref, yb_ref, yc_ref, yd_ref, w_ref, o_ref, *, w):
    acc = x_ref[...]
    for g, y_ref in enumerate((ya_ref, yb_ref, yc_ref, yd_ref)):
        acc = acc + jnp.dot(y_ref[...], w_ref[g * w:(g + 1) * w, :], preferred_element_type=F32)
    o_ref[...] = acc


def _out_proj(layer, x, ya, yb, yc, yd, w_out):
    t, d = x.shape
    w = ya.shape[1]
    tm = PROJ_TM
    xs = pl.BlockSpec((tm, d), lambda i, l: (i, 0))
    ys = pl.BlockSpec((tm, w), lambda i, l: (i, 0))
    return pl.pallas_call(
        functools.partial(_out_proj_body, w=w),
        grid_spec=pltpu.PrefetchScalarGridSpec(
            num_scalar_prefetch=1,
            grid=(t // tm,),
            in_specs=[xs, ys, ys, ys, ys,
                      pl.BlockSpec((None,) + w_out.shape[1:], lambda i, l: (l[0], 0, 0),
                                   pipeline_mode=pl.Buffered(1))],
            out_specs=xs,
        ),
        out_shape=jax.ShapeDtypeStruct((t, d), F32),
        compiler_params=_params("parallel"),
        name="out_proj",
    )(layer, x, ya, yb, yc, yd, w_out)


def _norm_body(x_ref, gain_ref, o_ref):
    o_ref[...] = _rms(x_ref[...], gain_ref[...])


def _final_norm(x, gain):
    t, d = x.shape
    tm = NORM_TM
    xs = pl.BlockSpec((tm, d), lambda i: (i, 0))
    return pl.pallas_call(
        _norm_body,
        grid=(t // tm,),
        in_specs=[xs, pl.BlockSpec((1, d), lambda i: (0, 0))],
        out_specs=xs,
        out_shape=jax.ShapeDtypeStruct((t, d), F32),
        compiler_params=_params("parallel"),
        name="final_norm",
    )(x, gain)


def _prep_ffn(wg, wu, wd):
    nl, d, f = wg.shape
    tf = FFN_TF
    nf = -(-f // tf)
    pad = nf * tf - f
    wg = jnp.pad(wg.astype(BF16), ((0, 0), (0, 0), (0, pad))).reshape(nl, d, nf, tf)
    wu = jnp.pad(wu.astype(BF16), ((0, 0), (0, 0), (0, pad))).reshape(nl, d, nf, tf)
    wgu = jnp.concatenate([wg, wu], axis=3).reshape(nl, d, nf * 2 * tf)
    wd = jnp.pad(wd.astype(BF16), ((0, 0), (0, pad), (0, 0)))
    return wgu, wd


def _trunk(x, p, n_layers):
    b, s, d = x.shape
    cos_t, sin_t = _dft_tables(s)

    def seq(a):
        return a.reshape(b, s, a.shape[-1])

    def layer_fn(li, x2):
        layer = jnp.full((1,), li, jnp.int32)
        x2 = _ffn(layer, x2, p["ln_ffn1"], p["wgu1"], p["wd1"])
        v1, v2, glu, sb, sg, *qkv = _in_proj(layer, x2, p["ln_mix"], p["w_in"], p["fold"])
        ya = _dft(cos_t, sin_t, seq(v1), seq(v2))
        yb, yd = _conv(layer, seq(glu), seq(sg), seq(sb), p["conv_b_w"], p["conv_b_bias"],
                       p["ln_conv_gain"], p["ln_conv_bias"], p["conv_d_w"])
        yc = _attn(qkv, b, s)
        flat = lambda a: a.reshape(b * s, a.shape[-1])
        x2 = _out_proj(layer, x2, flat(ya), flat(yb), yc, flat(yd), p["w_out"])
        return _ffn(layer, x2, p["ln_ffn2"], p["wgu2"], p["wd2"])

    x2 = lax.fori_loop(0, n_layers, layer_fn, x.reshape(b * s, d))
    return _final_norm(x2, p["ln_final"]).reshape(b, s, d)


def _prep_params(ln_ffn1, w_ffn1_gate, w_ffn1_up, w_ffn1_down, ln_mix, w_in, w_fourier, conv_b_w,
                 conv_b_bias, ln_conv_gain, ln_conv_bias, conv_d_w, w_out, ln_ffn2, w_ffn2_gate,
                 w_ffn2_up, w_ffn2_down, ln_final):
    row = lambda a: a[:, None, :]
    wgu1, wd1 = _prep_ffn(w_ffn1_gate, w_ffn1_up, w_ffn1_down)
    wgu2, wd2 = _prep_ffn(w_ffn2_gate, w_ffn2_up, w_ffn2_down)
    return dict(
        ln_ffn1=row(ln_ffn1), wgu1=wgu1, wd1=wd1, ln_mix=row(ln_mix), w_in=w_in.astype(BF16),
        fold=_fold_fourier(w_fourier), conv_b_w=conv_b_w, conv_b_bias=row(conv_b_bias),
        ln_conv_gain=row(ln_conv_gain), ln_conv_bias=row(ln_conv_bias), conv_d_w=conv_d_w,
        w_out=w_out.astype(BF16), ln_ffn2=row(ln_ffn2), wgu2=wgu2, wd2=wd2,
        ln_final=ln_final[None, :])


def kernel(x_prompt, x_sample, ln_ffn1, w_ffn1_gate, w_ffn1_up, w_ffn1_down, ln_mix, w_in, w_fourier,
           conv_b_w, conv_b_bias, ln_conv_gain, ln_conv_bias, conv_d_w, w_out, ln_ffn2, w_ffn2_gate,
           w_ffn2_up, w_ffn2_down, ln_final):
    p = _prep_params(ln_ffn1, w_ffn1_gate, w_ffn1_up, w_ffn1_down, ln_mix, w_in, w_fourier, conv_b_w,
                     conv_b_bias, ln_conv_gain, ln_conv_bias, conv_d_w, w_out, ln_ffn2, w_ffn2_gate,
                     w_ffn2_up, w_ffn2_down, ln_final)
    n_layers = w_in.shape[0]
    return _trunk(x_prompt, p, n_layers), _trunk(x_sample, p, n_layers)
```

```python
import functools
import math

import jax
import jax.numpy as jnp
from jax import lax
from jax.experimental import pallas as pl
from jax.experimental.pallas import tpu as pltpu

F32 = jnp.float32
BF16 = jnp.bfloat16

N_FOURIER_GROUPS = 4
CONF_KERNEL = 31
SHORT_KERNEL = 3
HEAD_DIM = 64
DILATED_CONFIGS = ((128, 1), (512, 4), (2048, 16))
RMS_EPS = 1e-6
LN_EPS = 1e-5
NEG_BIG = -1e30

LANES = 128
SUBLANES_F32 = 8
SUBLANES_BF16 = 16
VMEM_LIMIT_BYTES = 60 * 1024 * 1024

FFN_TM = 1024
FFN_TF = 512
PROJ_TM = 512
DFT_TM = 1024
DFT_TK = 2048
DFT_BB = 2
DFT_GEN_ROWS = 128
CONV_TS = 512
CONV_HALO = 16
CONV_RC = 64
ATT_TQ = 256
ATT_SUB = 128


def _params(*sem):
    return pltpu.CompilerParams(dimension_semantics=sem, vmem_limit_bytes=VMEM_LIMIT_BYTES)


def _rms(x, gain):
    y = x * lax.rsqrt(jnp.mean(x * x, axis=-1, keepdims=True) + RMS_EPS)
    return y * gain


def _sigmoid(x):
    return 1.0 / (1.0 + jnp.exp(-x))


def _ffn_body(layer_ref, x_ref, gain_ref, wg_ref, wu_ref, wd_ref, *rest, last_layer):
    final_ref = rest[0] if last_layer is not None else None
    o_ref, xn_ref = rest[-2:]
    j = pl.program_id(1)

    @pl.when(j == 0)
    def _():
        x = x_ref[...]
        xn_ref[...] = _rms(x, gain_ref[...]).astype(BF16)
        o_ref[...] = x

    xn = xn_ref[...]
    g = jnp.dot(xn, wg_ref[...], preferred_element_type=F32)
    u = jnp.dot(xn, wu_ref[...], preferred_element_type=F32)
    h = (g * _sigmoid(g)) * u * 0.5
    o_ref[...] += jnp.dot(h.astype(BF16), wd_ref[...], preferred_element_type=F32)

    if last_layer is not None:
        @pl.when(jnp.logical_and(j == pl.num_programs(1) - 1, layer_ref[0] == last_layer))
        def _():
            o_ref[...] = _rms(o_ref[...], final_ref[...])


def _ffn(layer, x, gain, wg, wu, wd, final_gain=None, last_layer=None):
    t, d = x.shape
    tf = FFN_TF
    nf = wd.shape[1] // tf
    tm = FFN_TM
    up = pl.BlockSpec((None, d, tf), lambda i, j, l: (l[0], 0, j))
    in_specs = [
        pl.BlockSpec((tm, d), lambda i, j, l: (i, 0)),
        pl.BlockSpec((None, 1, d), lambda i, j, l: (l[0], 0, 0)),
        up, up,
        pl.BlockSpec((None, tf, d), lambda i, j, l: (l[0], j, 0)),
    ]
    args = [layer, x, gain, wg, wu, wd]
    if final_gain is not None:
        in_specs.append(pl.BlockSpec((1, d), lambda i, j, l: (0, 0)))
        args.append(final_gain)
    return pl.pallas_call(
        functools.partial(_ffn_body, last_layer=None if final_gain is None else last_layer),
        grid_spec=pltpu.PrefetchScalarGridSpec(
            num_scalar_prefetch=1,
            grid=(t // tm, nf),
            in_specs=in_specs,
            out_specs=pl.BlockSpec((tm, d), lambda i, j, l: (i, 0)),
            scratch_shapes=[pltpu.VMEM((tm, d), BF16)],
        ),
        out_shape=jax.ShapeDtypeStruct((t, d), F32),
        compiler_params=_params("parallel", "arbitrary"),
        name="ffn",
    )(*args)


def _fold_body(w_ref, o_ref):
    n = w_ref.shape[0]
    r = lax.broadcasted_iota(jnp.int32, (n, n), 0)
    c = lax.broadcasted_iota(jnp.int32, (n, n), 1)
    k = (r * c) & (n - 1)
    k = jnp.where(k >= n // 2, k - n, k)
    ang = k.astype(F32) * (2.0 * math.pi / n)
    w = w_ref[...]
    scale = n ** -0.5
    a = jnp.dot(jnp.cos(ang), w, preferred_element_type=F32, precision=lax.Precision.HIGHEST)
    b = jnp.dot(jnp.sin(ang), w, preferred_element_type=F32, precision=lax.Precision.HIGHEST)
    o_ref[:, :n] = (a * scale).astype(BF16)
    o_ref[:, n:] = (b * -scale).astype(BF16)


def _fold_fourier(w_fourier):
    nl, ng, c, _ = w_fourier.shape
    return pl.pallas_call(
        _fold_body,
        grid=(nl, ng),
        in_specs=[pl.BlockSpec((None, None, c, c), lambda l, g: (l, g, 0, 0))],
        out_specs=pl.BlockSpec((None, None, c, 2 * c), lambda l, g: (l, g, 0, 0)),
        out_shape=jax.ShapeDtypeStruct((nl, ng, c, 2 * c), BF16),
        compiler_params=_params("parallel", "parallel"),
        name="fold_fourier",
    )(w_fourier)


def _dft_gen_body(cos_ref, sin_ref, cb_ref, sb_ref, *, s, rows):
    i = pl.program_id(0)

    def angle(prod):
        k = prod & (s - 1)
        k = jnp.where(k >= s // 2, k - s, k)
        return k.astype(F32) * (2.0 * math.pi / s)

    @pl.when(i == 0)
    def _():
        r = lax.broadcasted_iota(jnp.int32, (rows, s), 0)
        c = lax.broadcasted_iota(jnp.int32, (rows, s), 1)
        b = angle(r * c)
        cb_ref[...] = jnp.cos(b)
        sb_ref[...] = jnp.sin(b)

    a = angle((i * rows) * lax.broadcasted_iota(jnp.int32, (1, s), 1))
    ca = jnp.cos(a)
    sa = jnp.sin(a)
    cb = cb_ref[...]
    sb = sb_ref[...]
    cos_ref[...] = (ca * cb - sa * sb).astype(BF16)
    sin_ref[...] = (sa * cb + ca * sb).astype(BF16)


def _dft_tables(s):
    assert s & (s - 1) == 0 and (s - 1) ** 2 < 2 ** 31
    rows = DFT_GEN_ROWS
    spec = pl.BlockSpec((rows, s), lambda i: (i, 0))
    return pl.pallas_call(
        functools.partial(_dft_gen_body, s=s, rows=rows),
        grid=(s // rows,),
        in_specs=[],
        out_specs=[spec, spec],
        out_shape=[jax.ShapeDtypeStruct((s, s), BF16)] * 2,
        scratch_shapes=[pltpu.VMEM((rows, s), F32)] * 2,
        compiler_params=_params("arbitrary"),
        name="dft_tables",
    )()


def _dft_body(c_ref, s_ref, v1_ref, v2_ref, o_ref, acc_ref, *, bb, scale):
    kk = pl.program_id(2)

    @pl.when(kk == 0)
    def _():
        acc_ref[...] = jnp.zeros_like(acc_ref)

    for b in range(bb):
        acc_ref[b] += (jnp.dot(c_ref[...], v1_ref[b], preferred_element_type=F32)
                       + jnp.dot(s_ref[...], v2_ref[b], preferred_element_type=F32))

    @pl.when(kk == pl.num_programs(2) - 1)
    def _():
        o_ref[...] = (acc_ref[...] * scale).astype(BF16)


def _dft(cos_t, sin_t, v1, v2):
    b, s, w = v1.shape
    bb, tm, tk = DFT_BB, min(DFT_TM, s), min(DFT_TK, s)
    tab = pl.BlockSpec((tm, tk), lambda n, i, k: (i, k))
    vin = pl.BlockSpec((bb, tk, w), lambda n, i, k: (n, k, 0))
    return pl.pallas_call(
        functools.partial(_dft_body, bb=bb, scale=s ** -0.5),
        grid=(b // bb, s // tm, s // tk),
        in_specs=[tab, tab, vin, vin],
        out_specs=pl.BlockSpec((bb, tm, w), lambda n, i, k: (n, i, 0)),
        out_shape=jax.ShapeDtypeStruct((b, s, w), BF16),
        scratch_shapes=[pltpu.VMEM((bb, tm, w), F32)],
        compiler_params=_params("parallel", "parallel", "arbitrary"),
        name="dft",
    )(cos_t, sin_t, v1, v2)


def _in_proj_body(layer_ref, x_ref, gain_ref, w_ref, fold_ref, v1_ref, v2_ref, glu_ref, sb_ref, sg_ref,
                  *rest, w, ng, dils):
    qkv_refs, pbuf = rest[:-1], rest[-1]
    xn = _rms(x_ref[...], gain_ref[...]).astype(BF16)

    def proj(g):
        return jnp.dot(xn, w_ref[:, g * w:(g + 1) * w], preferred_element_type=F32)

    ua = proj(0).astype(BF16)
    c = w // ng
    for g in range(ng):
        vv = jnp.dot(ua[:, g * c:(g + 1) * c], fold_ref[g], preferred_element_type=F32)
        v1_ref[:, g * c:(g + 1) * c] = vv[:, :c].astype(BF16)
        v2_ref[:, g * c:(g + 1) * c] = vv[:, c:].astype(BF16)
    glu_ref[...] = proj(1) * _sigmoid(proj(2))
    tm = x_ref.shape[0]
    for n, scale in enumerate((HEAD_DIM ** -0.5, 1.0, 1.0)):
        val = proj(3 + n) * scale
        for c in range(w // LANES):
            pbuf[n, c] = val[:, c * LANES:(c + 1) * LANES]
        for m, dil in enumerate(dils):
            out = qkv_refs[n * len(dils) + m]
            if dil == 1:
                out[...] = val.astype(BF16)
                continue
            for r in range(dil):
                for c in range(w // LANES):
                    rows = pbuf[n, c, pl.ds(r, tm // dil, stride=dil), :]
                    out[:, r * w + c * LANES:r * w + (c + 1) * LANES] = rows.astype(BF16)
    sb_ref[...] = proj(6)
    sg_ref[...] = proj(7) * proj(8)


def _in_proj(layer, x, gain, w_in, fold):
    t, d = x.shape
    ng, c = fold.shape[1], fold.shape[2]
    w = ng * c
    tm = PROJ_TM
    dils = tuple(dil for _, dil in DILATED_CONFIGS)
    assert all(tm % (dil * SUBLANES_BF16) == 0 for dil in dils)
    out = pl.BlockSpec((tm, w), lambda i, l: (i, 0))
    dts = (BF16, BF16, F32, F32, F32)
    views = [(tm // dil, dil * w) for _ in range(3) for dil in dils]
    return pl.pallas_call(
        functools.partial(_in_proj_body, w=w, ng=ng, dils=dils),
        grid_spec=pltpu.PrefetchScalarGridSpec(
            num_scalar_prefetch=1,
            grid=(t // tm,),
            in_specs=[
                pl.BlockSpec((tm, d), lambda i, l: (i, 0)),
                pl.BlockSpec((None, 1, d), lambda i, l: (l[0], 0, 0)),
                pl.BlockSpec((None, d, w_in.shape[2]), lambda i, l: (l[0], 0, 0),
                             pipeline_mode=pl.Buffered(1)),
                pl.BlockSpec((None, ng, c, 2 * c), lambda i, l: (l[0], 0, 0, 0)),
            ],
            out_specs=[out] * len(dts) + [pl.BlockSpec(blk, lambda i, l: (i, 0)) for blk in views],
            scratch_shapes=[pltpu.VMEM((3, w // LANES, tm, LANES), F32)],
        ),
        out_shape=([jax.ShapeDtypeStruct((t, w), dt) for dt in dts]
                   + [jax.ShapeDtypeStruct((t // tm * blk[0], blk[1]), BF16) for blk in views]),
        compiler_params=_params("parallel"),
        name="in_proj",
    )(layer, x, gain, w_in, fold)


def _conv_body(layer_ref, h_ref, hp_ref, hn_ref, g_ref, gp_ref, gn_ref, sb_ref,
               wb_ref, bias_ref, lng_ref, lnb_ref, wd_ref, yb_ref, yd_ref, hbuf, gbuf, hsh, gsh, *, ts):
    i = pl.program_id(1)
    first = i == 0
    last = i == pl.num_programs(1) - 1
    halo = CONV_HALO
    for buf, prev, cur, nxt in ((hbuf, hp_ref, h_ref, hn_ref), (gbuf, gp_ref, g_ref, gn_ref)):
        buf[0:halo] = jnp.where(first, 0.0, prev[...])
        buf[halo:halo + ts] = cur[...]
        buf[halo + ts:] = jnp.where(last, 0.0, nxt[...])

    kb, kd = CONF_KERNEL, SHORT_KERNEL
    span = ts + 2 * halo - SUBLANES_F32
    h_shifts = tuple(range(1, SUBLANES_F32))
    g_shifts = tuple(sorted({(halo - kd // 2 + k) % SUBLANES_F32 for k in range(kd)} - {0}))
    for buf, shifted, shifts in ((hbuf, hsh, h_shifts), (gbuf, gsh, g_shifts)):
        for j, sh in enumerate(shifts):
            shifted[j, 0:span] = buf[sh:sh + span]

    def window(buf, shifted, shifts, start):
        sh = start % SUBLANES_F32
        if sh == 0:
            return buf[start:start + CONV_RC, :]
        return shifted[shifts.index(sh), start - sh:start - sh + CONV_RC, :]

    for r0 in range(0, ts, CONV_RC):
        rows = slice(r0, r0 + CONV_RC)
        acc = jnp.zeros((CONV_RC, h_ref.shape[-1]), F32) + bias_ref[...]
        for k in range(kb):
            acc = acc + wb_ref[k:k + 1, :] * window(hbuf, hsh, h_shifts, r0 + halo - kb // 2 + k)
        mu = jnp.mean(acc, axis=-1, keepdims=True)
        cen = acc - mu
        var = jnp.mean(cen * cen, axis=-1, keepdims=True)
        y = cen * lax.rsqrt(var + LN_EPS) * lng_ref[...] + lnb_ref[...]
        yb_ref[rows, :] = (y * _sigmoid(y)).astype(BF16)

        acd = jnp.zeros((CONV_RC, g_ref.shape[-1]), F32)
        for k in range(kd):
            acd = acd + wd_ref[k:k + 1, :] * window(gbuf, gsh, g_shifts, r0 + halo - kd // 2 + k)
        yd_ref[rows, :] = (sb_ref[rows, :] * acd).astype(BF16)


def _conv(layer, glu, sg, sb, conv_b_w, conv_b_bias, ln_g, ln_b, conv_d_w):
    b, s, w = glu.shape
    ts, halo = CONV_TS, CONV_HALO
    nh = ts // halo
    main = pl.BlockSpec((None, ts, w), lambda n, i, l: (n, i, 0))
    prev = pl.BlockSpec((None, halo, w), lambda n, i, l: (n, jnp.maximum(i * nh - 1, 0), 0))
    nxt = pl.BlockSpec((None, halo, w), lambda n, i, l: (n, jnp.minimum((i + 1) * nh, s // halo - 1), 0))

    def per_layer(a):
        return pl.BlockSpec((None,) + a.shape[1:], lambda n, i, l: (l[0], 0, 0))

    return pl.pallas_call(
        functools.partial(_conv_body, ts=ts),
        grid_spec=pltpu.PrefetchScalarGridSpec(
            num_scalar_prefetch=1,
            grid=(b, s // ts),
            in_specs=[main, prev, nxt, main, prev, nxt, main,
                      per_layer(conv_b_w), per_layer(conv_b_bias), per_layer(ln_g), per_layer(ln_b),
                      per_layer(conv_d_w)],
            out_specs=[main, main],
            scratch_shapes=[pltpu.VMEM((ts + 2 * halo, w), F32)] * 2 + [
                pltpu.VMEM((SUBLANES_F32 - 1, ts + 2 * halo, w), F32),
                pltpu.VMEM((min(SHORT_KERNEL, SUBLANES_F32) - 1, ts + 2 * halo, w), F32)],
        ),
        out_shape=[jax.ShapeDtypeStruct((b, s, w), BF16)] * 2,
        compiler_params=_params("parallel", "parallel"),
        name="conv",
    )(layer, glu, glu, glu, sg, sg, sg, sb, conv_b_w, conv_b_bias, ln_g, ln_b, conv_d_w)


def _band_attn_body(q_ref, kp_ref, k_ref, kn_ref, vp_ref, v_ref, vn_ref, acc_ref, st_ref, kbuf, vbuf,
                    *, tq, sub, half, dil, nl, nh):
    i = pl.program_id(1)
    res_class = pl.program_id(2)
    for buf, prev, cur, nxt in ((kbuf, kp_ref, k_ref, kn_ref), (vbuf, vp_ref, v_ref, vn_ref)):
        buf[0:half] = prev[...]
        buf[half:half + tq] = cur[...]
        buf[half + tq:] = nxt[...]

    win = sub + 2 * half
    e = HEAD_DIM
    lane = lax.broadcasted_iota(jnp.int32, (sub, 2 * e), 1)
    st_lane = lax.broadcasted_iota(jnp.int32, (sub, st_ref.shape[-1]), 1)
    for so in range(0, tq, sub):
        r = lax.broadcasted_iota(jnp.int32, (sub, win), 0)
        c = lax.broadcasted_iota(jnp.int32, (sub, win), 1)
        ad = jnp.abs(r + half - c)
        key_row = i * tq + (so - half) + c
        valid = jnp.logical_and(ad <= half, jnp.logical_and(key_row >= 0, key_row < nl))
        dist = (ad * dil).astype(F32)
        stats = jnp.zeros(st_lane.shape, F32)
        for hp in range(nh // 2):
            lanes = slice(hp * 2 * e, (hp + 1) * 2 * e)
            qp = q_ref[so:so + sub, lanes]
            kw = kbuf[so:so + win, lanes]
            vw = vbuf[so:so + win, lanes]
            res = None
            for hh in range(2):
                h = 2 * hp + hh
                mine = (lane < e) if hh == 0 else (lane >= e)
                qh = jnp.where(mine, qp, jnp.zeros_like(qp))
                s = lax.dot_general(qh, kw, (((1,), (1,)), ((), ())), preferred_element_type=F32)
                slope = 2.0 ** (-8.0 * (h + 1) / nh)
                s = jnp.where(valid, s - slope * dist, NEG_BIG)
                m = jnp.max(s, axis=-1, keepdims=True)
                p = jnp.exp(s - m)
                den = jnp.sum(p, axis=-1, keepdims=True)
                pv = jnp.dot(p.astype(BF16), vw, preferred_element_type=F32)
                res = pv if res is None else jnp.where(mine, pv, res)
                stats = jnp.where(st_lane == h, m, stats)
                stats = jnp.where(st_lane == nh + h, den, stats)
            if dil == 1:
                acc_ref[hp, so:so + sub, :] = res
            else:
                acc_ref[hp, pl.ds(so * dil + res_class, sub, stride=dil), :] = res
        if dil == 1:
            st_ref[so:so + sub, :] = stats
        else:
            st_ref[pl.ds(so * dil + res_class, sub, stride=dil), :] = stats


def _band_attn(q, k, v, b, s, window, dil):
    w = q.shape[1] // dil
    half = window // (2 * dil)
    nl = s // dil
    assert s % dil == 0 and nl % half == 0 and half % SUBLANES_BF16 == 0
    tq = min(ATT_TQ, nl)
    sub = min(ATT_SUB, tq)
    nh = w // HEAD_DIM
    hb = tq // half
    view = lambda a: a.reshape(b, nl, dil * w)
    main = pl.BlockSpec((None, tq, w), lambda n, i, r: (n, i, r))
    prev = pl.BlockSpec((None, half, w), lambda n, i, r: (n, jnp.maximum(i * hb - 1, 0), r))
    nxt = pl.BlockSpec((None, half, w), lambda n, i, r: (n, jnp.minimum((i + 1) * hb, nl // half - 1), r))
    acc, st = pl.pallas_call(
        functools.partial(_band_attn_body, tq=tq, sub=sub, half=half, dil=dil, nl=nl, nh=nh),
        grid=(b, nl // tq, dil),
        in_specs=[main, prev, main, nxt, prev, main, nxt],
        out_specs=[pl.BlockSpec((None, nh // 2, tq * dil, 2 * HEAD_DIM), lambda n, i, r: (n, 0, i, 0)),
                   pl.BlockSpec((None, tq * dil, LANES), lambda n, i, r: (n, i, 0))],
        out_shape=[jax.ShapeDtypeStruct((b, nh // 2, s, 2 * HEAD_DIM), F32),
                   jax.ShapeDtypeStruct((b, s, LANES), F32)],
        scratch_shapes=[pltpu.VMEM((tq + 2 * half, w), BF16)] * 2,
        compiler_params=_params("parallel", "parallel", "arbitrary"),
        name="band_attn",
    )(view(q), view(k), view(k), view(k), view(v), view(v), view(v))
    return acc, st


def _merge_branches(accs, st_refs, yc_ref, nh):
    sts = [st[...] for st in st_refs]
    nlane = sts[0].shape[-1]
    lane = lax.broadcasted_iota(jnp.int32, sts[0].shape, 1)
    m_all = functools.reduce(jnp.maximum, sts)
    ws = [jnp.exp(st - m_all) for st in sts]
    dens = [pltpu.roll(st, nlane - nh, 1) for st in sts]
    total = sum(wt * dn for wt, dn in zip(ws, dens))
    w_out = yc_ref.shape[-1]
    spread = (lax.broadcasted_iota(jnp.int32, (nlane, w_out), 0)
              == lax.broadcasted_iota(jnp.int32, (nlane, w_out), 1) // HEAD_DIM).astype(BF16)
    wide = []
    for wt in ws:
        wn = jnp.where(lane < nh, wt / total, 0.0)
        hi = wn.astype(BF16)
        lo = (wn - hi.astype(F32)).astype(BF16)
        wide.append(jnp.dot(hi, spread, preferred_element_type=F32)
                    + jnp.dot(lo, spread, preferred_element_type=F32))
    pw = accs[0].shape[-1]
    for hp in range(accs[0].shape[0]):
        cols = slice(hp * pw, (hp + 1) * pw)
        out = sum(acc[hp] * wd[:, cols] for acc, wd in zip(accs, wide))
        yc_ref[:, cols] = out.astype(BF16)


def _attn_branches(qkv, b, s):
    nd = len(DILATED_CONFIGS)
    return [_band_attn(qkv[m], qkv[nd + m], qkv[2 * nd + m], b, s, window, dil)
            for m, (window, dil) in enumerate(DILATED_CONFIGS)]


def _out_proj_body(layer_ref, x_ref, ya_ref, yb_ref, yd_ref, *rest, w, nb):
    accs, st_refs = rest[:nb], rest[nb:2 * nb]
    w_ref, o_ref, yc_ref = rest[2 * nb:]
    _merge_branches(accs, st_refs, yc_ref, w // HEAD_DIM)
    acc = x_ref[...]
    for g, y_ref in enumerate((ya_ref, yb_ref, yc_ref, yd_ref)):
        acc = acc + jnp.dot(y_ref[...], w_ref[g * w:(g + 1) * w, :], preferred_element_type=F32)
    o_ref[...] = acc


def _out_proj(layer, x, ya, yb, yd, branches, w_out, s):
    t, d = x.shape
    w = ya.shape[1]
    tm = PROJ_TM
    spt = s // tm
    accs = [br[0] for br in branches]
    sts = [br[1] for br in branches]
    _, npair, _, pw = accs[0].shape
    xs = pl.BlockSpec((tm, d), lambda i, l: (i, 0))
    ys = pl.BlockSpec((tm, w), lambda i, l: (i, 0))
    a_spec = pl.BlockSpec((None, npair, tm, pw), lambda i, l: (i // spt, 0, i % spt, 0))
    s_spec = pl.BlockSpec((None, tm, LANES), lambda i, l: (i // spt, i % spt, 0))
    return pl.pallas_call(
        functools.partial(_out_proj_body, w=w, nb=len(accs)),
        grid_spec=pltpu.PrefetchScalarGridSpec(
            num_scalar_prefetch=1,
            grid=(t // tm,),
            in_specs=[xs, ys, ys, ys] + [a_spec] * len(accs) + [s_spec] * len(sts) + [
                pl.BlockSpec((None,) + w_out.shape[1:], lambda i, l: (l[0], 0, 0),
                             pipeline_mode=pl.Buffered(1))],
            out_specs=xs,
            scratch_shapes=[pltpu.VMEM((tm, w), BF16)],
        ),
        out_shape=jax.ShapeDtypeStruct((t, d), F32),
        compiler_params=_params("parallel"),
        name="out_proj",
    )(layer, x, ya, yb, yd, *accs, *sts, w_out)


def _prep_ffn(wg, wu, wd):
    f = wg.shape[2]
    pad = -f % FFN_TF
    return (jnp.pad(wg.astype(BF16), ((0, 0), (0, 0), (0, pad))),
            jnp.pad(wu.astype(BF16), ((0, 0), (0, 0), (0, pad))),
            jnp.pad(wd.astype(BF16), ((0, 0), (0, pad), (0, 0))))


def _trunk(x, p, n_layers):
    b, s, d = x.shape
    cos_t, sin_t = _dft_tables(s)

    def seq(a):
        return a.reshape(b, s, a.shape[-1])

    def layer_fn(li, x2):
        layer = jnp.full((1,), li, jnp.int32)
        x2 = _ffn(layer, x2, p["ln_ffn1"], *p["ffn1"])
        v1, v2, glu, sb, sg, *qkv = _in_proj(layer, x2, p["ln_mix"], p["w_in"], p["fold"])
        ya = _dft(cos_t, sin_t, seq(v1), seq(v2))
        yb, yd = _conv(layer, seq(glu), seq(sg), seq(sb), p["conv_b_w"], p["conv_b_bias"],
                       p["ln_conv_gain"], p["ln_conv_bias"], p["conv_d_w"])
        flat = lambda a: a.reshape(b * s, a.shape[-1])
        x2 = _out_proj(layer, x2, flat(ya), flat(yb), flat(yd), _attn_branches(qkv, b, s), p["w_out"], s)
        return _ffn(layer, x2, p["ln_ffn2"], *p["ffn2"], final_gain=p["ln_final"], last_layer=n_layers - 1)

    return lax.fori_loop(0, n_layers, layer_fn, x.reshape(b * s, d)).reshape(b, s, d)


def _prep_params(ln_ffn1, w_ffn1_gate, w_ffn1_up, w_ffn1_down, ln_mix, w_in, w_fourier, conv_b_w,
                 conv_b_bias, ln_conv_gain, ln_conv_bias, conv_d_w, w_out, ln_ffn2, w_ffn2_gate,
                 w_ffn2_up, w_ffn2_down, ln_final):
    row = lambda a: a[:, None, :]
    return dict(
        ln_ffn1=row(ln_ffn1), ffn1=_prep_ffn(w_ffn1_gate, w_ffn1_up, w_ffn1_down),
        ln_mix=row(ln_mix), w_in=w_in.astype(BF16),
        fold=_fold_fourier(w_fourier), conv_b_w=conv_b_w, conv_b_bias=row(conv_b_bias),
        ln_conv_gain=row(ln_conv_gain), ln_conv_bias=row(ln_conv_bias), conv_d_w=conv_d_w,
        w_out=w_out.astype(BF16), ln_ffn2=row(ln_ffn2),
        ffn2=_prep_ffn(w_ffn2_gate, w_ffn2_up, w_ffn2_down), ln_final=ln_final[None, :])


def kernel(x_prompt, x_sample, ln_ffn1, w_ffn1_gate, w_ffn1_up, w_ffn1_down, ln_mix, w_in, w_fourier,
           conv_b_w, conv_b_bias, ln_conv_gain, ln_conv_bias, conv_d_w, w_out, ln_ffn2, w_ffn2_gate,
           w_ffn2_up, w_ffn2_down, ln_final):
    p = _prep_params(ln_ffn1, w_ffn1_gate, w_ffn1_up, w_ffn1_down, ln_mix, w_in, w_fourier, conv_b_w,
                     conv_b_bias, ln_conv_gain, ln_conv_bias, conv_d_w, w_out, ln_ffn2, w_ffn2_gate,
                     w_ffn2_up, w_ffn2_down, ln_final)
    n_layers = w_in.shape[0]
    return _trunk(x_prompt, p, n_layers), _trunk(x_sample, p, n_layers)
```

```python
import functools
import math

import jax
import jax.numpy as jnp
from jax import lax
from jax.experimental import pallas as pl
from jax.experimental.pallas import tpu as pltpu

F32 = jnp.float32
BF16 = jnp.bfloat16

N_FOURIER_GROUPS = 4
CONF_KERNEL = 31
SHORT_KERNEL = 3
HEAD_DIM = 64
DILATED_CONFIGS = ((128, 1), (512, 4), (2048, 16))
RMS_EPS = 1e-6
LN_EPS = 1e-5
NEG_BIG = -1e30

LANES = 128
SUBLANES_F32 = 8
SUBLANES_BF16 = 16
VMEM_LIMIT_BYTES = 60 * 1024 * 1024

FFN_TM = 1024
FFN_TF = 512
PROJ_TM = 512
DFT_TM = 1024
DFT_TK = 2048
DFT_BB = 2
DFT_GEN_ROWS = 128
CONV_TS = 512
CONV_HALO = 16
CONV_RC = 64
ATT_TQ = 512
ATT_SUB = 128
CAST_TILE = 256


def _params(*sem):
    return pltpu.CompilerParams(dimension_semantics=sem, vmem_limit_bytes=VMEM_LIMIT_BYTES)


def _rms(x, gain):
    y = x * lax.rsqrt(jnp.mean(x * x, axis=-1, keepdims=True) + RMS_EPS)
    return y * gain


def _sigmoid(x):
    return 1.0 / (1.0 + jnp.exp(-x))


def _ffn_body(layer_ref, x_ref, gain_ref, wg_ref, wu_ref, wd_ref, *rest, last_layer):
    final_ref = rest[0] if last_layer is not None else None
    o_ref, xn_ref = rest[-2:]
    j = pl.program_id(1)

    def tile_contribution():
        xn = xn_ref[...]
        g = jnp.dot(xn, wg_ref[...], preferred_element_type=F32)
        u = jnp.dot(xn, wu_ref[...], preferred_element_type=F32)
        h = (g * _sigmoid(g)) * u * 0.5
        return jnp.dot(h.astype(BF16), wd_ref[...], preferred_element_type=F32)

    @pl.when(j == 0)
    def _():
        xn_ref[...] = _rms(x_ref[...], gain_ref[...]).astype(BF16)
        o_ref[...] = x_ref[...] + tile_contribution()

    @pl.when(j > 0)
    def _():
        o_ref[...] += tile_contribution()

    if last_layer is not None:
        @pl.when(jnp.logical_and(j == pl.num_programs(1) - 1, layer_ref[0] == last_layer))
        def _():
            o_ref[...] = _rms(o_ref[...], final_ref[...])


def _ffn(layer, x, gain, wg, wu, wd, final_gain=None, last_layer=None):
    t, d = x.shape
    tf = FFN_TF
    nf = wd.shape[1] // tf
    tm = FFN_TM
    up = pl.BlockSpec((None, d, tf), lambda i, j, l: (l[0], 0, j))
    in_specs = [
        pl.BlockSpec((tm, d), lambda i, j, l: (i, 0)),
        pl.BlockSpec((None, 1, d), lambda i, j, l: (l[0], 0, 0)),
        up, up,
        pl.BlockSpec((None, tf, d), lambda i, j, l: (l[0], j, 0)),
    ]
    args = [layer, x, gain, wg, wu, wd]
    if final_gain is not None:
        in_specs.append(pl.BlockSpec((1, d), lambda i, j, l: (0, 0)))
        args.append(final_gain)
    return pl.pallas_call(
        functools.partial(_ffn_body, last_layer=None if final_gain is None else last_layer),
        grid_spec=pltpu.PrefetchScalarGridSpec(
            num_scalar_prefetch=1,
            grid=(t // tm, nf),
            in_specs=in_specs,
            out_specs=pl.BlockSpec((tm, d), lambda i, j, l: (i, 0)),
            scratch_shapes=[pltpu.VMEM((tm, d), BF16)],
        ),
        out_shape=jax.ShapeDtypeStruct((t, d), F32),
        compiler_params=_params("parallel", "arbitrary"),
        name="ffn",
    )(*args)


def _fold_body(w_ref, o_ref):
    n = w_ref.shape[0]
    r = lax.broadcasted_iota(jnp.int32, (n, n), 0)
    c = lax.broadcasted_iota(jnp.int32, (n, n), 1)
    k = (r * c) & (n - 1)
    k = jnp.where(k >= n // 2, k - n, k)
    ang = k.astype(F32) * (2.0 * math.pi / n)
    w = w_ref[...]
    scale = n ** -0.5
    a = jnp.dot(jnp.cos(ang), w, preferred_element_type=F32, precision=lax.Precision.HIGHEST)
    b = jnp.dot(jnp.sin(ang), w, preferred_element_type=F32, precision=lax.Precision.HIGHEST)
    o_ref[:, :n] = (a * scale).astype(BF16)
    o_ref[:, n:] = (b * -scale).astype(BF16)


def _fold_fourier(w_fourier):
    nl, ng, c, _ = w_fourier.shape
    return pl.pallas_call(
        _fold_body,
        grid=(nl, ng),
        in_specs=[pl.BlockSpec((None, None, c, c), lambda l, g: (l, g, 0, 0))],
        out_specs=pl.BlockSpec((None, None, c, 2 * c), lambda l, g: (l, g, 0, 0)),
        out_shape=jax.ShapeDtypeStruct((nl, ng, c, 2 * c), BF16),
        compiler_params=_params("parallel", "parallel"),
        name="fold_fourier",
    )(w_fourier)


def _dft_gen_body(cos_ref, sin_ref, cb_ref, sb_ref, *, s, rows):
    i = pl.program_id(0)

    def angle(prod):
        k = prod & (s - 1)
        k = jnp.where(k >= s // 2, k - s, k)
        return k.astype(F32) * (2.0 * math.pi / s)

    @pl.when(i == 0)
    def _():
        r = lax.broadcasted_iota(jnp.int32, (rows, s), 0)
        c = lax.broadcasted_iota(jnp.int32, (rows, s), 1)
        b = angle(r * c)
        cb_ref[...] = jnp.cos(b)
        sb_ref[...] = jnp.sin(b)

    a = angle((i * rows) * lax.broadcasted_iota(jnp.int32, (1, s), 1))
    ca = jnp.cos(a)
    sa = jnp.sin(a)
    cb = cb_ref[...]
    sb = sb_ref[...]
    cos_ref[...] = (ca * cb - sa * sb).astype(BF16)
    sin_ref[...] = (sa * cb + ca * sb).astype(BF16)


def _dft_tables(s):
    assert s & (s - 1) == 0 and (s - 1) ** 2 < 2 ** 31
    rows = DFT_GEN_ROWS
    spec = pl.BlockSpec((rows, s), lambda i: (i, 0))
    return pl.pallas_call(
        functools.partial(_dft_gen_body, s=s, rows=rows),
        grid=(s // rows,),
        in_specs=[],
        out_specs=[spec, spec],
        out_shape=[jax.ShapeDtypeStruct((s, s), BF16)] * 2,
        scratch_shapes=[pltpu.VMEM((rows, s), F32)] * 2,
        compiler_params=_params("arbitrary"),
        name="dft_tables",
    )()


def _dft_body(c_ref, s_ref, v1_ref, v2_ref, o_ref, acc_ref, *, bb, scale):
    kk = pl.program_id(2)

    @pl.when(kk == 0)
    def _():
        acc_ref[...] = jnp.zeros_like(acc_ref)

    for b in range(bb):
        acc_ref[b] += (jnp.dot(c_ref[...], v1_ref[b], preferred_element_type=F32)
                       + jnp.dot(s_ref[...], v2_ref[b], preferred_element_type=F32))

    @pl.when(kk == pl.num_programs(2) - 1)
    def _():
        o_ref[...] = (acc_ref[...] * scale).astype(BF16)


def _dft(cos_t, sin_t, v1, v2):
    b, s, w = v1.shape
    bb, tm, tk = DFT_BB, min(DFT_TM, s), min(DFT_TK, s)
    tab = pl.BlockSpec((tm, tk), lambda n, i, k: (i, k))
    vin = pl.BlockSpec((bb, tk, w), lambda n, i, k: (n, k, 0))
    return pl.pallas_call(
        functools.partial(_dft_body, bb=bb, scale=s ** -0.5),
        grid=(b // bb, s // tm, s // tk),
        in_specs=[tab, tab, vin, vin],
        out_specs=pl.BlockSpec((bb, tm, w), lambda n, i, k: (n, i, 0)),
        out_shape=jax.ShapeDtypeStruct((b, s, w), BF16),
        scratch_shapes=[pltpu.VMEM((bb, tm, w), F32)],
        compiler_params=_params("parallel", "parallel", "arbitrary"),
        name="dft",
    )(cos_t, sin_t, v1, v2)


def _in_proj_body(layer_ref, x_ref, gain_ref, w_ref, fold_ref, v1_ref, v2_ref, glu_ref, sb_ref, sg_ref,
                  *rest, w, ng, dils):
    qkv_refs, pbuf = rest[:-1], rest[-1]
    xn = _rms(x_ref[...], gain_ref[...]).astype(BF16)

    def proj(g):
        return jnp.dot(xn, w_ref[:, g * w:(g + 1) * w], preferred_element_type=F32)

    ua = proj(0).astype(BF16)
    c = w // ng
    for g in range(ng):
        vv = jnp.dot(ua[:, g * c:(g + 1) * c], fold_ref[g], preferred_element_type=F32)
        v1_ref[:, g * c:(g + 1) * c] = vv[:, :c].astype(BF16)
        v2_ref[:, g * c:(g + 1) * c] = vv[:, c:].astype(BF16)
    glu_ref[...] = proj(1) * _sigmoid(proj(2))
    tm = x_ref.shape[0]
    for n, scale in enumerate((HEAD_DIM ** -0.5, 1.0, 1.0)):
        val = proj(3 + n) * scale
        for c in range(w // LANES):
            pbuf[n, c] = val[:, c * LANES:(c + 1) * LANES]
        for m, dil in enumerate(dils):
            out = qkv_refs[n * len(dils) + m]
            if dil == 1:
                out[...] = val.astype(BF16)
                continue
            for r in range(dil):
                for c in range(w // LANES):
                    rows = pbuf[n, c, pl.ds(r, tm // dil, stride=dil), :]
                    out[:, r * w + c * LANES:r * w + (c + 1) * LANES] = rows.astype(BF16)
    sb_ref[...] = proj(6)
    sg_ref[...] = proj(7) * proj(8)


def _in_proj(layer, x, gain, w_in, fold):
    t, d = x.shape
    ng, c = fold.shape[1], fold.shape[2]
    w = ng * c
    tm = PROJ_TM
    dils = tuple(dil for _, dil in DILATED_CONFIGS)
    assert all(tm % (dil * SUBLANES_BF16) == 0 for dil in dils)
    out = pl.BlockSpec((tm, w), lambda i, l: (i, 0))
    dts = (BF16, BF16, F32, F32, F32)
    views = [(tm // dil, dil * w) for _ in range(3) for dil in dils]
    return pl.pallas_call(
        functools.partial(_in_proj_body, w=w, ng=ng, dils=dils),
        grid_spec=pltpu.PrefetchScalarGridSpec(
            num_scalar_prefetch=1,
            grid=(t // tm,),
            in_specs=[
                pl.BlockSpec((tm, d), lambda i, l: (i, 0)),
                pl.BlockSpec((None, 1, d), lambda i, l: (l[0], 0, 0)),
                pl.BlockSpec((None, d, w_in.shape[2]), lambda i, l: (l[0], 0, 0),
                             pipeline_mode=pl.Buffered(1)),
                pl.BlockSpec((None, ng, c, 2 * c), lambda i, l: (l[0], 0, 0, 0)),
            ],
            out_specs=[out] * len(dts) + [pl.BlockSpec(blk, lambda i, l: (i, 0)) for blk in views],
            scratch_shapes=[pltpu.VMEM((3, w // LANES, tm, LANES), F32)],
        ),
        out_shape=([jax.ShapeDtypeStruct((t, w), dt) for dt in dts]
                   + [jax.ShapeDtypeStruct((t // tm * blk[0], blk[1]), BF16) for blk in views]),
        compiler_params=_params("parallel"),
        name="in_proj",
    )(layer, x, gain, w_in, fold)


def _conv_body(layer_ref, h_ref, hp_ref, hn_ref, g_ref, gp_ref, gn_ref, sb_ref,
               wb_ref, bias_ref, lng_ref, lnb_ref, wd_ref, yb_ref, yd_ref, hbuf, gbuf, hsh, gsh, *, ts):
    i = pl.program_id(1)
    first = i == 0
    last = i == pl.num_programs(1) - 1
    halo = CONV_HALO
    for buf, prev, cur, nxt in ((hbuf, hp_ref, h_ref, hn_ref), (gbuf, gp_ref, g_ref, gn_ref)):
        buf[0:halo] = jnp.where(first, 0.0, prev[...])
        buf[halo:halo + ts] = cur[...]
        buf[halo + ts:] = jnp.where(last, 0.0, nxt[...])

    kb, kd = CONF_KERNEL, SHORT_KERNEL
    span = ts + 2 * halo - SUBLANES_F32
    h_shifts = tuple(range(1, SUBLANES_F32))
    g_shifts = tuple(sorted({(halo - kd // 2 + k) % SUBLANES_F32 for k in range(kd)} - {0}))
    for buf, shifted, shifts in ((hbuf, hsh, h_shifts), (gbuf, gsh, g_shifts)):
        for j, sh in enumerate(shifts):
            shifted[j, 0:span] = buf[sh:sh + span]

    def window(buf, shifted, shifts, start):
        sh = start % SUBLANES_F32
        if sh == 0:
            return buf[start:start + CONV_RC, :]
        return shifted[shifts.index(sh), start - sh:start - sh + CONV_RC, :]

    for r0 in range(0, ts, CONV_RC):
        rows = slice(r0, r0 + CONV_RC)
        acc = jnp.zeros((CONV_RC, h_ref.shape[-1]), F32) + bias_ref[...]
        for k in range(kb):
            acc = acc + wb_ref[k:k + 1, :] * window(hbuf, hsh, h_shifts, r0 + halo - kb // 2 + k)
        mu = jnp.mean(acc, axis=-1, keepdims=True)
        cen = acc - mu
        var = jnp.mean(cen * cen, axis=-1, keepdims=True)
        y = cen * lax.rsqrt(var + LN_EPS) * lng_ref[...] + lnb_ref[...]
        yb_ref[rows, :] = (y * _sigmoid(y)).astype(BF16)

        acd = jnp.zeros((CONV_RC, g_ref.shape[-1]), F32)
        for k in range(kd):
            acd = acd + wd_ref[k:k + 1, :] * window(gbuf, gsh, g_shifts, r0 + halo - kd // 2 + k)
        yd_ref[rows, :] = (sb_ref[rows, :] * acd).astype(BF16)


def _conv(layer, glu, sg, sb, conv_b_w, conv_b_bias, ln_g, ln_b, conv_d_w):
    b, s, w = glu.shape
    ts, halo = CONV_TS, CONV_HALO
    nh = ts // halo
    main = pl.BlockSpec((None, ts, w), lambda n, i, l: (n, i, 0))
    prev = pl.BlockSpec((None, halo, w), lambda n, i, l: (n, jnp.maximum(i * nh - 1, 0), 0))
    nxt = pl.BlockSpec((None, halo, w), lambda n, i, l: (n, jnp.minimum((i + 1) * nh, s // halo - 1), 0))

    def per_layer(a):
        return pl.BlockSpec((None,) + a.shape[1:], lambda n, i, l: (l[0], 0, 0))

    return pl.pallas_call(
        functools.partial(_conv_body, ts=ts),
        grid_spec=pltpu.PrefetchScalarGridSpec(
            num_scalar_prefetch=1,
            grid=(b, s // ts),
            in_specs=[main, prev, nxt, main, prev, nxt, main,
                      per_layer(conv_b_w), per_layer(conv_b_bias), per_layer(ln_g), per_layer(ln_b),
                      per_layer(conv_d_w)],
            out_specs=[main, main],
            scratch_shapes=[pltpu.VMEM((ts + 2 * halo, w), F32)] * 2 + [
                pltpu.VMEM((SUBLANES_F32 - 1, ts + 2 * halo, w), F32),
                pltpu.VMEM((min(SHORT_KERNEL, SUBLANES_F32) - 1, ts + 2 * halo, w), F32)],
        ),
        out_shape=[jax.ShapeDtypeStruct((b, s, w), BF16)] * 2,
        compiler_params=_params("parallel", "parallel"),
        name="conv",
    )(layer, glu, glu, glu, sg, sg, sg, sb, conv_b_w, conv_b_bias, ln_g, ln_b, conv_d_w)


def _band_attn_body(q_ref, kp_ref, k_ref, kn_ref, vp_ref, v_ref, vn_ref, acc_ref, st_ref, kbuf, vbuf,
                    *, tq, sub, half, dil, nl, nh):
    i = pl.program_id(1)
    res_class = pl.program_id(2)
    for buf, prev, cur, nxt in ((kbuf, kp_ref, k_ref, kn_ref), (vbuf, vp_ref, v_ref, vn_ref)):
        buf[0:half] = prev[...]
        buf[half:half + tq] = cur[...]
        buf[half + tq:] = nxt[...]

    win = sub + 2 * half
    e = HEAD_DIM
    lane = lax.broadcasted_iota(jnp.int32, (sub, 2 * e), 1)
    st_lane = lax.broadcasted_iota(jnp.int32, (sub, st_ref.shape[-1]), 1)
    for so in range(0, tq, sub):
        r = lax.broadcasted_iota(jnp.int32, (sub, win), 0)
        c = lax.broadcasted_iota(jnp.int32, (sub, win), 1)
        ad = jnp.abs(r + half - c)
        key_row = i * tq + (so - half) + c
        valid = jnp.logical_and(ad <= half, jnp.logical_and(key_row >= 0, key_row < nl))
        dist = (ad * dil).astype(F32)
        stats = jnp.zeros(st_lane.shape, F32)
        for hp in range(nh // 2):
            lanes = slice(hp * 2 * e, (hp + 1) * 2 * e)
            qp = q_ref[so:so + sub, lanes]
            kw = kbuf[so:so + win, lanes]
            vw = vbuf[so:so + win, lanes]
            res = None
            for hh in range(2):
                h = 2 * hp + hh
                mine = (lane < e) if hh == 0 else (lane >= e)
                qh = jnp.where(mine, qp, jnp.zeros_like(qp))
                s = lax.dot_general(qh, kw, (((1,), (1,)), ((), ())), preferred_element_type=F32)
                slope = 2.0 ** (-8.0 * (h + 1) / nh)
                s = jnp.where(valid, s - slope * dist, NEG_BIG)
                m = jnp.max(s, axis=-1, keepdims=True)
                p = jnp.exp(s - m)
                den = jnp.sum(p, axis=-1, keepdims=True)
                pv = jnp.dot(p.astype(BF16), vw, preferred_element_type=F32)
                res = pv if res is None else jnp.where(mine, pv, res)
                stats = jnp.where(st_lane == h, m, stats)
                stats = jnp.where(st_lane == nh + h, den, stats)
            if dil == 1:
                acc_ref[hp, so:so + sub, :] = res
            else:
                acc_ref[hp, pl.ds(so * dil + res_class, sub, stride=dil), :] = res
        if dil == 1:
            st_ref[so:so + sub, :] = stats
        else:
            st_ref[pl.ds(so * dil + res_class, sub, stride=dil), :] = stats


def _band_attn(q, k, v, b, s, window, dil):
    w = q.shape[1] // dil
    half = window // (2 * dil)
    nl = s // dil
    assert s % dil == 0 and nl % half == 0 and half % SUBLANES_BF16 == 0
    tq = min(ATT_TQ, nl)
    sub = min(ATT_SUB, tq)
    nh = w // HEAD_DIM
    hb = tq // half
    view = lambda a: a.reshape(b, nl, dil * w)
    main = pl.BlockSpec((None, tq, w), lambda n, i, r: (n, i, r))
    prev = pl.BlockSpec((None, half, w), lambda n, i, r: (n, jnp.maximum(i * hb - 1, 0), r))
    nxt = pl.BlockSpec((None, half, w), lambda n, i, r: (n, jnp.minimum((i + 1) * hb, nl // half - 1), r))
    acc, st = pl.pallas_call(
        functools.partial(_band_attn_body, tq=tq, sub=sub, half=half, dil=dil, nl=nl, nh=nh),
        grid=(b, nl // tq, dil),
        in_specs=[main, prev, main, nxt, prev, main, nxt],
        out_specs=[pl.BlockSpec((None, nh // 2, tq * dil, 2 * HEAD_DIM), lambda n, i, r: (n, 0, i, 0)),
                   pl.BlockSpec((None, tq * dil, LANES), lambda n, i, r: (n, i, 0))],
        out_shape=[jax.ShapeDtypeStruct((b, nh // 2, s, 2 * HEAD_DIM), F32),
                   jax.ShapeDtypeStruct((b, s, LANES), F32)],
        scratch_shapes=[pltpu.VMEM((tq + 2 * half, w), BF16)] * 2,
        compiler_params=_params("parallel", "parallel", "arbitrary"),
        name="band_attn",
    )(view(q), view(k), view(k), view(k), view(v), view(v), view(v))
    return acc, st


def _merge_branches(accs, st_refs, yc_ref, nh):
    sts = [st[...] for st in st_refs]
    nlane = sts[0].shape[-1]
    lane = lax.broadcasted_iota(jnp.int32, sts[0].shape, 1)
    m_all = functools.reduce(jnp.maximum, sts)
    ws = [jnp.exp(st - m_all) for st in sts]
    dens = [pltpu.roll(st, nlane - nh, 1) for st in sts]
    total = sum(wt * dn for wt, dn in zip(ws, dens))
    w_out = yc_ref.shape[-1]
    spread = (lax.broadcasted_iota(jnp.int32, (nlane, w_out), 0)
              == lax.broadcasted_iota(jnp.int32, (nlane, w_out), 1) // HEAD_DIM).astype(BF16)
    wide = []
    for wt in ws:
        wn = jnp.where(lane < nh, wt / total, 0.0)
        hi = wn.astype(BF16)
        lo = (wn - hi.astype(F32)).astype(BF16)
        wide.append(jnp.dot(hi, spread, preferred_element_type=F32)
                    + jnp.dot(lo, spread, preferred_element_type=F32))
    pw = accs[0].shape[-1]
    for hp in range(accs[0].shape[0]):
        cols = slice(hp * pw, (hp + 1) * pw)
        out = sum(acc[hp] * wd[:, cols] for acc, wd in zip(accs, wide))
        yc_ref[:, cols] = out.astype(BF16)


def _attn_branches(qkv, b, s):
    nd = len(DILATED_CONFIGS)
    return [_band_attn(qkv[m], qkv[nd + m], qkv[2 * nd + m], b, s, window, dil)
            for m, (window, dil) in enumerate(DILATED_CONFIGS)]


def _out_proj_body(layer_ref, x_ref, ya_ref, yb_ref, yd_ref, *rest, w, nb):
    accs, st_refs = rest[:nb], rest[nb:2 * nb]
    w_ref, o_ref, yc_ref = rest[2 * nb:]
    _merge_branches(accs, st_refs, yc_ref, w // HEAD_DIM)
    acc = x_ref[...]
    for g, y_ref in enumerate((ya_ref, yb_ref, yc_ref, yd_ref)):
        acc = acc + jnp.dot(y_ref[...], w_ref[g * w:(g + 1) * w, :], preferred_element_type=F32)
    o_ref[...] = acc


def _out_proj(layer, x, ya, yb, yd, branches, w_out, s):
    t, d = x.shape
    w = ya.shape[1]
    tm = PROJ_TM
    spt = s // tm
    accs = [br[0] for br in branches]
    sts = [br[1] for br in branches]
    _, npair, _, pw = accs[0].shape
    xs = pl.BlockSpec((tm, d), lambda i, l: (i, 0))
    ys = pl.BlockSpec((tm, w), lambda i, l: (i, 0))
    a_spec = pl.BlockSpec((None, npair, tm, pw), lambda i, l: (i // spt, 0, i % spt, 0))
    s_spec = pl.BlockSpec((None, tm, LANES), lambda i, l: (i // spt, i % spt, 0))
    return pl.pallas_call(
        functools.partial(_out_proj_body, w=w, nb=len(accs)),
        grid_spec=pltpu.PrefetchScalarGridSpec(
            num_scalar_prefetch=1,
            grid=(t // tm,),
            in_specs=[xs, ys, ys, ys] + [a_spec] * len(accs) + [s_spec] * len(sts) + [
                pl.BlockSpec((None,) + w_out.shape[1:], lambda i, l: (l[0], 0, 0),
                             pipeline_mode=pl.Buffered(1))],
            out_specs=xs,
            scratch_shapes=[pltpu.VMEM((tm, w), BF16)],
        ),
        out_shape=jax.ShapeDtypeStruct((t, d), F32),
        compiler_params=_params("parallel"),
        name="out_proj",
    )(layer, x, ya, yb, yd, *accs, *sts, w_out)


def _cast_pad_body(w_ref, o_ref):
    r, c = w_ref.shape
    o_ref[:r, :c] = w_ref[...].astype(BF16)
    if o_ref.shape[0] > r:
        o_ref[r:, :] = jnp.zeros((o_ref.shape[0] - r, o_ref.shape[1]), BF16)
    if o_ref.shape[1] > c:
        o_ref[:, c:] = jnp.zeros((o_ref.shape[0], o_ref.shape[1] - c), BF16)


def _cast_pad(w, axis, mult):
    nl, r, c = w.shape
    tile = CAST_TILE
    if axis == 2:
        cp = c + (-c % mult)
        in_spec = pl.BlockSpec((None, tile, c), lambda l, i: (l, i, 0))
        out_spec = pl.BlockSpec((None, tile, cp), lambda l, i: (l, i, 0))
        grid, shape = (nl, r // tile), (nl, r, cp)
    else:
        rp = r + (-r % mult)
        in_spec = pl.BlockSpec((None, r, tile), lambda l, i: (l, 0, i))
        out_spec = pl.BlockSpec((None, rp, tile), lambda l, i: (l, 0, i))
        grid, shape = (nl, c // tile), (nl, rp, c)
    return pl.pallas_call(
        _cast_pad_body,
        grid=grid,
        in_specs=[in_spec],
        out_specs=out_spec,
        out_shape=jax.ShapeDtypeStruct(shape, BF16),
        compiler_params=_params("parallel", "parallel"),
        name="cast_pad",
    )(w)


def _prep_ffn(wg, wu, wd):
    return _cast_pad(wg, 2, FFN_TF), _cast_pad(wu, 2, FFN_TF), _cast_pad(wd, 1, FFN_TF)


def _trunk(x, p, n_layers):
    b, s, d = x.shape
    cos_t, sin_t = _dft_tables(s)

    def seq(a):
        return a.reshape(b, s, a.shape[-1])

    def layer_fn(li, x2):
        layer = jnp.full((1,), li, jnp.int32)
        x2 = _ffn(layer, x2, p["ln_ffn1"], *p["ffn1"])
        v1, v2, glu, sb, sg, *qkv = _in_proj(layer, x2, p["ln_mix"], p["w_in"], p["fold"])
        ya = _dft(cos_t, sin_t, seq(v1), seq(v2))
        yb, yd = _conv(layer, seq(glu), seq(sg), seq(sb), p["conv_b_w"], p["conv_b_bias"],
                       p["ln_conv_gain"], p["ln_conv_bias"], p["conv_d_w"])
        flat = lambda a: a.reshape(b * s, a.shape[-1])
        x2 = _out_proj(layer, x2, flat(ya), flat(yb), flat(yd), _attn_branches(qkv, b, s), p["w_out"], s)
        return _ffn(layer, x2, p["ln_ffn2"], *p["ffn2"], final_gain=p["ln_final"], last_layer=n_layers - 1)

    x2 = x.reshape(b * s, d)
    for li in range(n_layers):
        x2 = layer_fn(li, x2)
    return x2.reshape(b, s, d)


def _prep_params(ln_ffn1, w_ffn1_gate, w_ffn1_up, w_ffn1_down, ln_mix, w_in, w_fourier, conv_b_w,
                 conv_b_bias, ln_conv_gain, ln_conv_bias, conv_d_w, w_out, ln_ffn2, w_ffn2_gate,
                 w_ffn2_up, w_ffn2_down, ln_final):
    row = lambda a: a[:, None, :]
    return dict(
        ln_ffn1=row(ln_ffn1), ffn1=_prep_ffn(w_ffn1_gate, w_ffn1_up, w_ffn1_down),
        ln_mix=row(ln_mix), w_in=w_in.astype(BF16),
        fold=_fold_fourier(w_fourier), conv_b_w=conv_b_w, conv_b_bias=row(conv_b_bias),
        ln_conv_gain=row(ln_conv_gain), ln_conv_bias=row(ln_conv_bias), conv_d_w=conv_d_w,
        w_out=w_out.astype(BF16), ln_ffn2=row(ln_ffn2),
        ffn2=_prep_ffn(w_ffn2_gate, w_ffn2_up, w_ffn2_down), ln_final=ln_final[None, :])


def kernel(x_prompt, x_sample, ln_ffn1, w_ffn1_gate, w_ffn1_up, w_ffn1_down, ln_mix, w_in, w_fourier,
           conv_b_w, conv_b_bias, ln_conv_gain, ln_conv_bias, conv_d_w, w_out, ln_ffn2, w_ffn2_gate,
           w_ffn2_up, w_ffn2_down, ln_final):
    p = _prep_params(ln_ffn1, w_ffn1_gate, w_ffn1_up, w_ffn1_down, ln_mix, w_in, w_fourier, conv_b_w,
                     conv_b_bias, ln_conv_gain, ln_conv_bias, conv_d_w, w_out, ln_ffn2, w_ffn2_gate,
                     w_ffn2_up, w_ffn2_down, ln_final)
    n_layers = w_in.shape[0]
    return _trunk(x_prompt, p, n_layers), _trunk(x_sample, p, n_layers)
```

```python
import functools
import math

import jax
import jax.numpy as jnp
from jax import lax
from jax.experimental import pallas as pl
from jax.experimental.pallas import tpu as pltpu

F32 = jnp.float32
BF16 = jnp.bfloat16

N_FOURIER_GROUPS = 4
CONF_KERNEL = 31
SHORT_KERNEL = 3
HEAD_DIM = 64
DILATED_CONFIGS = ((128, 1), (512, 4), (2048, 16))
RMS_EPS = 1e-6
LN_EPS = 1e-5
NEG_BIG = -1e30

LANES = 128
SUBLANES_F32 = 8
SUBLANES_BF16 = 16
VMEM_LIMIT_BYTES = 60 * 1024 * 1024

FFN_TM = 1024
FFN_TF = 512
PROJ_TM = 512
DFT_TM = 1024
DFT_TK = 2048
DFT_BB = 2
DFT_GEN_ROWS = 128
CONV_TS = 512
CONV_HALO = 16
CONV_RC = 64
ATT_TQ = 512
ATT_SUB = 128
CAST_TILE = 256


def _params(*sem):
    return pltpu.CompilerParams(dimension_semantics=sem, vmem_limit_bytes=VMEM_LIMIT_BYTES)


def _rms(x, gain):
    y = x * lax.rsqrt(jnp.mean(x * x, axis=-1, keepdims=True) + RMS_EPS)
    return y * gain


def _sigmoid(x):
    return 1.0 / (1.0 + jnp.exp(-x))


def _ffn_body(layer_ref, x_ref, gain_ref, wg_ref, wu_ref, wd_ref, *rest, last_layer):
    final_ref = rest[0] if last_layer is not None else None
    o_ref, xn_ref = rest[-2:]
    j = pl.program_id(1)

    def tile_contribution():
        xn = xn_ref[...]
        g = jnp.dot(xn, wg_ref[...], preferred_element_type=F32)
        u = jnp.dot(xn, wu_ref[...], preferred_element_type=F32)
        h = (g * _sigmoid(g)) * u * 0.5
        return jnp.dot(h.astype(BF16), wd_ref[...], preferred_element_type=F32)

    @pl.when(j == 0)
    def _():
        xn_ref[...] = _rms(x_ref[...], gain_ref[...]).astype(BF16)
        o_ref[...] = x_ref[...] + tile_contribution()

    @pl.when(j > 0)
    def _():
        o_ref[...] += tile_contribution()

    if last_layer is not None:
        @pl.when(jnp.logical_and(j == pl.num_programs(1) - 1, layer_ref[0] == last_layer))
        def _():
            o_ref[...] = _rms(o_ref[...], final_ref[...])


def _ffn(layer, x, gain, wg, wu, wd, final_gain=None, last_layer=None):
    t, d = x.shape
    tf = FFN_TF
    nf = wd.shape[1] // tf
    tm = FFN_TM
    up = pl.BlockSpec((None, d, tf), lambda i, j, l: (l[0], 0, j))
    in_specs = [
        pl.BlockSpec((tm, d), lambda i, j, l: (i, 0)),
        pl.BlockSpec((None, 1, d), lambda i, j, l: (l[0], 0, 0)),
        up, up,
        pl.BlockSpec((None, tf, d), lambda i, j, l: (l[0], j, 0)),
    ]
    args = [layer, x, gain, wg, wu, wd]
    if final_gain is not None:
        in_specs.append(pl.BlockSpec((1, d), lambda i, j, l: (0, 0)))
        args.append(final_gain)
    return pl.pallas_call(
        functools.partial(_ffn_body, last_layer=None if final_gain is None else last_layer),
        grid_spec=pltpu.PrefetchScalarGridSpec(
            num_scalar_prefetch=1,
            grid=(t // tm, nf),
            in_specs=in_specs,
            out_specs=pl.BlockSpec((tm, d), lambda i, j, l: (i, 0)),
            scratch_shapes=[pltpu.VMEM((tm, d), BF16)],
        ),
        out_shape=jax.ShapeDtypeStruct((t, d), F32),
        compiler_params=_params("parallel", "arbitrary"),
        name="ffn",
    )(*args)


def _fold_body(w_ref, o_ref):
    n = w_ref.shape[0]
    r = lax.broadcasted_iota(jnp.int32, (n, n), 0)
    c = lax.broadcasted_iota(jnp.int32, (n, n), 1)
    k = (r * c) & (n - 1)
    k = jnp.where(k >= n // 2, k - n, k)
    ang = k.astype(F32) * (2.0 * math.pi / n)
    w = w_ref[...]
    scale = n ** -0.5
    a = jnp.dot(jnp.cos(ang), w, preferred_element_type=F32, precision=lax.Precision.HIGHEST)
    b = jnp.dot(jnp.sin(ang), w, preferred_element_type=F32, precision=lax.Precision.HIGHEST)
    o_ref[:, :n] = (a * scale).astype(BF16)
    o_ref[:, n:] = (b * -scale).astype(BF16)


def _fold_fourier(w_fourier):
    nl, ng, c, _ = w_fourier.shape
    return pl.pallas_call(
        _fold_body,
        grid=(nl, ng),
        in_specs=[pl.BlockSpec((None, None, c, c), lambda l, g: (l, g, 0, 0))],
        out_specs=pl.BlockSpec((None, None, c, 2 * c), lambda l, g: (l, g, 0, 0)),
        out_shape=jax.ShapeDtypeStruct((nl, ng, c, 2 * c), BF16),
        compiler_params=_params("parallel", "parallel"),
        name="fold_fourier",
    )(w_fourier)


def _dft_gen_body(cos_ref, sin_ref, cb_ref, sb_ref, *, s, rows):
    i = pl.program_id(0)

    def angle(prod):
        k = prod & (s - 1)
        k = jnp.where(k >= s // 2, k - s, k)
        return k.astype(F32) * (2.0 * math.pi / s)

    @pl.when(i == 0)
    def _():
        r = lax.broadcasted_iota(jnp.int32, (rows, s), 0)
        c = lax.broadcasted_iota(jnp.int32, (rows, s), 1)
        b = angle(r * c)
        cb_ref[...] = jnp.cos(b)
        sb_ref[...] = jnp.sin(b)

    a = angle((i * rows) * lax.broadcasted_iota(jnp.int32, (1, s), 1))
    ca = jnp.cos(a)
    sa = jnp.sin(a)
    cb = cb_ref[...]
    sb = sb_ref[...]
    cos_ref[...] = (ca * cb - sa * sb).astype(BF16)
    sin_ref[...] = (sa * cb + ca * sb).astype(BF16)


def _dft_tables(s):
    assert s & (s - 1) == 0 and (s - 1) ** 2 < 2 ** 31
    rows = DFT_GEN_ROWS
    spec = pl.BlockSpec((rows, s), lambda i: (i, 0))
    return pl.pallas_call(
        functools.partial(_dft_gen_body, s=s, rows=rows),
        grid=(s // rows,),
        in_specs=[],
        out_specs=[spec, spec],
        out_shape=[jax.ShapeDtypeStruct((s, s), BF16)] * 2,
        scratch_shapes=[pltpu.VMEM((rows, s), F32)] * 2,
        compiler_params=_params("arbitrary"),
        name="dft_tables",
    )()


def _dft_body(c_ref, s_ref, v1_ref, v2_ref, o_ref, acc_ref, *, bb, scale):
    kk = pl.program_id(2)

    @pl.when(kk == 0)
    def _():
        acc_ref[...] = jnp.zeros_like(acc_ref)

    for b in range(bb):
        acc_ref[b] += (jnp.dot(c_ref[...], v1_ref[b], preferred_element_type=F32)
                       + jnp.dot(s_ref[...], v2_ref[b], preferred_element_type=F32))

    @pl.when(kk == pl.num_programs(2) - 1)
    def _():
        o_ref[...] = (acc_ref[...] * scale).astype(BF16)


def _dft(cos_t, sin_t, v1, v2):
    b, s, w = v1.shape
    bb, tm, tk = DFT_BB, min(DFT_TM, s), min(DFT_TK, s)
    tab = pl.BlockSpec((tm, tk), lambda n, i, k: (i, k))
    vin = pl.BlockSpec((bb, tk, w), lambda n, i, k: (n, k, 0))
    return pl.pallas_call(
        functools.partial(_dft_body, bb=bb, scale=s ** -0.5),
        grid=(b // bb, s // tm, s // tk),
        in_specs=[tab, tab, vin, vin],
        out_specs=pl.BlockSpec((bb, tm, w), lambda n, i, k: (n, i, 0)),
        out_shape=jax.ShapeDtypeStruct((b, s, w), BF16),
        scratch_shapes=[pltpu.VMEM((bb, tm, w), F32)],
        compiler_params=_params("parallel", "parallel", "arbitrary"),
        name="dft",
    )(cos_t, sin_t, v1, v2)


def _in_proj_body(layer_ref, x_ref, gain_ref, w_ref, fold_ref, v1_ref, v2_ref, glu_ref, sb_ref, sg_ref,
                  *rest, w, ng, dils):
    qkv_refs, pbuf = rest[:-1], rest[-1]
    xn = _rms(x_ref[...], gain_ref[...]).astype(BF16)

    def proj(g):
        return jnp.dot(xn, w_ref[:, g * w:(g + 1) * w], preferred_element_type=F32)

    ua = proj(0).astype(BF16)
    c = w // ng
    for g in range(ng):
        vv = jnp.dot(ua[:, g * c:(g + 1) * c], fold_ref[g], preferred_element_type=F32)
        v1_ref[:, g * c:(g + 1) * c] = vv[:, :c].astype(BF16)
        v2_ref[:, g * c:(g + 1) * c] = vv[:, c:].astype(BF16)
    glu_ref[...] = proj(1) * _sigmoid(proj(2))
    tm = x_ref.shape[0]
    for n, scale in enumerate((HEAD_DIM ** -0.5, 1.0, 1.0)):
        val = proj(3 + n) * scale
        for c in range(w // LANES):
            pbuf[0, n, c] = val[:, c * LANES:(c + 1) * LANES]
        for m, dil in enumerate(dils):
            out = qkv_refs[n * len(dils) + m]
            if m == 0:
                out[...] = val.astype(BF16)
                continue
            coarse = dils[m - 1]
            ratio, n_prev, n_cur = dil // coarse, tm // coarse, tm // dil
            for r_prev in range(coarse):
                for q in range(ratio):
                    r = r_prev + coarse * q
                    for c in range(w // LANES):
                        rows = pbuf[m - 1, n, c, pl.ds(r_prev * n_prev + q, n_cur, stride=ratio), :]
                        out[:, r * w + c * LANES:r * w + (c + 1) * LANES] = rows.astype(BF16)
                        if m + 1 < len(dils):
                            pbuf[m, n, c, r * n_cur:(r + 1) * n_cur, :] = rows
    sb_ref[...] = proj(6)
    sg_ref[...] = proj(7) * proj(8)


def _in_proj(layer, x, gain, w_in, fold):
    t, d = x.shape
    ng, c = fold.shape[1], fold.shape[2]
    w = ng * c
    tm = PROJ_TM
    dils = tuple(dil for _, dil in DILATED_CONFIGS)
    assert dils[0] == 1 and all(tm % (dil * SUBLANES_BF16) == 0 for dil in dils)
    assert all(fine % coarse == 0 for coarse, fine in zip(dils, dils[1:]))
    out = pl.BlockSpec((tm, w), lambda i, l: (i, 0))
    dts = (BF16, BF16, F32, F32, F32)
    views = [(tm // dil, dil * w) for _ in range(3) for dil in dils]
    return pl.pallas_call(
        functools.partial(_in_proj_body, w=w, ng=ng, dils=dils),
        grid_spec=pltpu.PrefetchScalarGridSpec(
            num_scalar_prefetch=1,
            grid=(t // tm,),
            in_specs=[
                pl.BlockSpec((tm, d), lambda i, l: (i, 0)),
                pl.BlockSpec((None, 1, d), lambda i, l: (l[0], 0, 0)),
                pl.BlockSpec((None, d, w_in.shape[2]), lambda i, l: (l[0], 0, 0),
                             pipeline_mode=pl.Buffered(1)),
                pl.BlockSpec((None, ng, c, 2 * c), lambda i, l: (l[0], 0, 0, 0)),
            ],
            out_specs=[out] * len(dts) + [pl.BlockSpec(blk, lambda i, l: (i, 0)) for blk in views],
            scratch_shapes=[pltpu.VMEM((len(dils) - 1, 3, w // LANES, tm, LANES), F32)],
        ),
        out_shape=([jax.ShapeDtypeStruct((t, w), dt) for dt in dts]
                   + [jax.ShapeDtypeStruct((t // tm * blk[0], blk[1]), BF16) for blk in views]),
        compiler_params=_params("parallel"),
        name="in_proj",
    )(layer, x, gain, w_in, fold)


def _conv_body(layer_ref, h_ref, hp_ref, hn_ref, g_ref, gp_ref, gn_ref, sb_ref,
               wb_ref, bias_ref, lng_ref, lnb_ref, wd_ref, yb_ref, yd_ref, hbuf, gbuf, hsh, gsh, *, ts):
    i = pl.program_id(1)
    first = i == 0
    last = i == pl.num_programs(1) - 1
    halo = CONV_HALO
    for buf, prev, cur, nxt in ((hbuf, hp_ref, h_ref, hn_ref), (gbuf, gp_ref, g_ref, gn_ref)):
        buf[0:halo] = jnp.where(first, 0.0, prev[...])
        buf[halo:halo + ts] = cur[...]
        buf[halo + ts:] = jnp.where(last, 0.0, nxt[...])

    kb, kd = CONF_KERNEL, SHORT_KERNEL
    span = ts + 2 * halo - SUBLANES_F32
    h_shifts = tuple(range(1, SUBLANES_F32))
    g_shifts = tuple(sorted({(halo - kd // 2 + k) % SUBLANES_F32 for k in range(kd)} - {0}))
    for buf, shifted, shifts in ((hbuf, hsh, h_shifts), (gbuf, gsh, g_shifts)):
        for j, sh in enumerate(shifts):
            shifted[j, 0:span] = buf[sh:sh + span]

    def window(buf, shifted, shifts, start):
        sh = start % SUBLANES_F32
        if sh == 0:
            return buf[start:start + CONV_RC, :]
        return shifted[shifts.index(sh), start - sh:start - sh + CONV_RC, :]

    for r0 in range(0, ts, CONV_RC):
        rows = slice(r0, r0 + CONV_RC)
        acc = jnp.zeros((CONV_RC, h_ref.shape[-1]), F32) + bias_ref[...]
        for k in range(kb):
            acc = acc + wb_ref[k:k + 1, :] * window(hbuf, hsh, h_shifts, r0 + halo - kb // 2 + k)
        mu = jnp.mean(acc, axis=-1, keepdims=True)
        cen = acc - mu
        var = jnp.mean(cen * cen, axis=-1, keepdims=True)
        y = cen * lax.rsqrt(var + LN_EPS) * lng_ref[...] + lnb_ref[...]
        yb_ref[rows, :] = (y * _sigmoid(y)).astype(BF16)

        acd = jnp.zeros((CONV_RC, g_ref.shape[-1]), F32)
        for k in range(kd):
            acd = acd + wd_ref[k:k + 1, :] * window(gbuf, gsh, g_shifts, r0 + halo - kd // 2 + k)
        yd_ref[rows, :] = (sb_ref[rows, :] * acd).astype(BF16)


def _conv(layer, glu, sg, sb, conv_b_w, conv_b_bias, ln_g, ln_b, conv_d_w):
    b, s, w = glu.shape
    ts, halo = CONV_TS, CONV_HALO
    nh = ts // halo
    main = pl.BlockSpec((None, ts, w), lambda n, i, l: (n, i, 0))
    prev = pl.BlockSpec((None, halo, w), lambda n, i, l: (n, jnp.maximum(i * nh - 1, 0), 0))
    nxt = pl.BlockSpec((None, halo, w), lambda n, i, l: (n, jnp.minimum((i + 1) * nh, s // halo - 1), 0))

    def per_layer(a):
        return pl.BlockSpec((None,) + a.shape[1:], lambda n, i, l: (l[0], 0, 0))

    return pl.pallas_call(
        functools.partial(_conv_body, ts=ts),
        grid_spec=pltpu.PrefetchScalarGridSpec(
            num_scalar_prefetch=1,
            grid=(b, s // ts),
            in_specs=[main, prev, nxt, main, prev, nxt, main,
                      per_layer(conv_b_w), per_layer(conv_b_bias), per_layer(ln_g), per_layer(ln_b),
                      per_layer(conv_d_w)],
            out_specs=[main, main],
            scratch_shapes=[pltpu.VMEM((ts + 2 * halo, w), F32)] * 2 + [
                pltpu.VMEM((SUBLANES_F32 - 1, ts + 2 * halo, w), F32),
                pltpu.VMEM((min(SHORT_KERNEL, SUBLANES_F32) - 1, ts + 2 * halo, w), F32)],
        ),
        out_shape=[jax.ShapeDtypeStruct((b, s, w), BF16)] * 2,
        compiler_params=_params("parallel", "parallel"),
        name="conv",
    )(layer, glu, glu, glu, sg, sg, sg, sb, conv_b_w, conv_b_bias, ln_g, ln_b, conv_d_w)


def _band_attn_body(q_ref, kp_ref, k_ref, kn_ref, vp_ref, v_ref, vn_ref, acc_ref, st_ref, kbuf, vbuf,
                    *, tq, sub, half, dil, nl, nh, nc):
    i = pl.program_id(1)
    for buf, prev, cur, nxt in ((kbuf, kp_ref, k_ref, kn_ref), (vbuf, vp_ref, v_ref, vn_ref)):
        buf[0:half] = prev[...]
        buf[half:half + tq] = cur[...]
        buf[half + tq:] = nxt[...]

    win = sub + 2 * half
    e = HEAD_DIM
    w = nh * e
    lane = lax.broadcasted_iota(jnp.int32, (sub, 2 * e), 1)
    st_lane = lax.broadcasted_iota(jnp.int32, (sub, st_ref.shape[-1]), 1)
    for so in range(0, tq, sub):
        r = lax.broadcasted_iota(jnp.int32, (sub, win), 0)
        c = lax.broadcasted_iota(jnp.int32, (sub, win), 1)
        ad = jnp.abs(r + half - c)
        key_row = i * tq + (so - half) + c
        valid = jnp.logical_and(ad <= half, jnp.logical_and(key_row >= 0, key_row < nl))
        dist = (ad * dil).astype(F32)
        for cls in range(nc):
            res_class = pl.program_id(2) * nc + cls
            stats = jnp.zeros(st_lane.shape, F32)
            for hp in range(nh // 2):
                lanes = slice(cls * w + hp * 2 * e, cls * w + (hp + 1) * 2 * e)
                qp = q_ref[so:so + sub, lanes]
                kw = kbuf[so:so + win, lanes]
                vw = vbuf[so:so + win, lanes]
                res = None
                for hh in range(2):
                    h = 2 * hp + hh
                    mine = (lane < e) if hh == 0 else (lane >= e)
                    qh = jnp.where(mine, qp, jnp.zeros_like(qp))
                    s = lax.dot_general(qh, kw, (((1,), (1,)), ((), ())), preferred_element_type=F32)
                    slope = 2.0 ** (-8.0 * (h + 1) / nh)
                    s = jnp.where(valid, s - slope * dist, NEG_BIG)
                    m = jnp.max(s, axis=-1, keepdims=True)
                    p = jnp.exp(s - m)
                    den = jnp.sum(p, axis=-1, keepdims=True)
                    pv = jnp.dot(p.astype(BF16), vw, preferred_element_type=F32)
                    res = pv if res is None else jnp.where(mine, pv, res)
                    stats = jnp.where(st_lane == h, m, stats)
                    stats = jnp.where(st_lane == nh + h, den, stats)
                if dil == 1:
                    acc_ref[hp, so:so + sub, :] = res
                else:
                    acc_ref[hp, pl.ds(so * dil + res_class, sub, stride=dil), :] = res
            if dil == 1:
                st_ref[so:so + sub, :] = stats
            else:
                st_ref[pl.ds(so * dil + res_class, sub, stride=dil), :] = stats


def _band_attn(q, k, v, b, s, window, dil):
    w = q.shape[1] // dil
    half = window // (2 * dil)
    nl = s // dil
    assert s % dil == 0 and nl % half == 0 and half % SUBLANES_BF16 == 0
    tq = min(ATT_TQ, nl)
    sub = min(ATT_SUB, tq)
    nc = min(dil, ATT_TQ // tq)
    nh = w // HEAD_DIM
    hb = tq // half
    view = lambda a: a.reshape(b, nl, dil * w)
    main = pl.BlockSpec((None, tq, nc * w), lambda n, i, r: (n, i, r))
    prev = pl.BlockSpec((None, half, nc * w), lambda n, i, r: (n, jnp.maximum(i * hb - 1, 0), r))
    nxt = pl.BlockSpec((None, half, nc * w),
                       lambda n, i, r: (n, jnp.minimum((i + 1) * hb, nl // half - 1), r))
    acc, st = pl.pallas_call(
        functools.partial(_band_attn_body, tq=tq, sub=sub, half=half, dil=dil, nl=nl, nh=nh, nc=nc),
        grid=(b, nl // tq, dil // nc),
        in_specs=[main, prev, main, nxt, prev, main, nxt],
        out_specs=[pl.BlockSpec((None, nh // 2, tq * dil, 2 * HEAD_DIM), lambda n, i, r: (n, 0, i, 0)),
                   pl.BlockSpec((None, tq * dil, LANES), lambda n, i, r: (n, i, 0))],
        out_shape=[jax.ShapeDtypeStruct((b, nh // 2, s, 2 * HEAD_DIM), F32),
                   jax.ShapeDtypeStruct((b, s, LANES), F32)],
        scratch_shapes=[pltpu.VMEM((tq + 2 * half, nc * w), BF16)] * 2,
        compiler_params=_params("parallel", "parallel", "arbitrary"),
        name="band_attn",
    )(view(q), view(k), view(k), view(k), view(v), view(v), view(v))
    return acc, st


def _merge_branches(accs, st_refs, yc_ref, nh):
    sts = [st[...] for st in st_refs]
    nlane = sts[0].shape[-1]
    lane = lax.broadcasted_iota(jnp.int32, sts[0].shape, 1)
    m_all = functools.reduce(jnp.maximum, sts)
    ws = [jnp.exp(st - m_all) for st in sts]
    dens = [pltpu.roll(st, nlane - nh, 1) for st in sts]
    total = sum(wt * dn for wt, dn in zip(ws, dens))
    w_out = yc_ref.shape[-1]
    spread = (lax.broadcasted_iota(jnp.int32, (nlane, w_out), 0)
              == lax.broadcasted_iota(jnp.int32, (nlane, w_out), 1) // HEAD_DIM).astype(BF16)
    wide = []
    for wt in ws:
        wn = jnp.where(lane < nh, wt / total, 0.0)
        hi = wn.astype(BF16)
        lo = (wn - hi.astype(F32)).astype(BF16)
        wide.append(jnp.dot(hi, spread, preferred_element_type=F32)
                    + jnp.dot(lo, spread, preferred_element_type=F32))
    pw = accs[0].shape[-1]
    for hp in range(accs[0].shape[0]):
        cols = slice(hp * pw, (hp + 1) * pw)
        out = sum(acc[hp] * wd[:, cols] for acc, wd in zip(accs, wide))
        yc_ref[:, cols] = out.astype(BF16)


def _attn_branches(qkv, b, s):
    nd = len(DILATED_CONFIGS)
    return [_band_attn(qkv[m], qkv[nd + m], qkv[2 * nd + m], b, s, window, dil)
            for m, (window, dil) in enumerate(DILATED_CONFIGS)]


def _out_proj_body(layer_ref, x_ref, ya_ref, yb_ref, yd_ref, *rest, w, nb):
    accs, st_refs = rest[:nb], rest[nb:2 * nb]
    w_ref, o_ref, yc_ref = rest[2 * nb:]
    _merge_branches(accs, st_refs, yc_ref, w // HEAD_DIM)
    acc = x_ref[...]
    for g, y_ref in enumerate((ya_ref, yb_ref, yc_ref, yd_ref)):
        acc = acc + jnp.dot(y_ref[...], w_ref[g * w:(g + 1) * w, :], preferred_element_type=F32)
    o_ref[...] = acc


def _out_proj(layer, x, ya, yb, yd, branches, w_out, s):
    t, d = x.shape
    w = ya.shape[1]
    tm = PROJ_TM
    spt = s // tm
    accs = [br[0] for br in branches]
    sts = [br[1] for br in branches]
    _, npair, _, pw = accs[0].shape
    xs = pl.BlockSpec((tm, d), lambda i, l: (i, 0))
    ys = pl.BlockSpec((tm, w), lambda i, l: (i, 0))
    a_spec = pl.BlockSpec((None, npair, tm, pw), lambda i, l: (i // spt, 0, i % spt, 0))
    s_spec = pl.BlockSpec((None, tm, LANES), lambda i, l: (i // spt, i % spt, 0))
    return pl.pallas_call(
        functools.partial(_out_proj_body, w=w, nb=len(accs)),
        grid_spec=pltpu.PrefetchScalarGridSpec(
            num_scalar_prefetch=1,
            grid=(t // tm,),
            in_specs=[xs, ys, ys, ys] + [a_spec] * len(accs) + [s_spec] * len(sts) + [
                pl.BlockSpec((None,) + w_out.shape[1:], lambda i, l: (l[0], 0, 0),
                             pipeline_mode=pl.Buffered(1))],
            out_specs=xs,
            scratch_shapes=[pltpu.VMEM((tm, w), BF16)],
        ),
        out_shape=jax.ShapeDtypeStruct((t, d), F32),
        compiler_params=_params("parallel"),
        name="out_proj",
    )(layer, x, ya, yb, yd, *accs, *sts, w_out)


def _cast_pad_body(w_ref, o_ref):
    r, c = w_ref.shape
    o_ref[:r, :c] = w_ref[...].astype(BF16)
    if o_ref.shape[0] > r:
        o_ref[r:, :] = jnp.zeros((o_ref.shape[0] - r, o_ref.shape[1]), BF16)
    if o_ref.shape[1] > c:
        o_ref[:, c:] = jnp.zeros((o_ref.shape[0], o_ref.shape[1] - c), BF16)


def _cast_pad(w, axis, mult):
    nl, r, c = w.shape
    tile = CAST_TILE
    if axis == 2:
        cp = c + (-c % mult)
        in_spec = pl.BlockSpec((None, tile, c), lambda l, i: (l, i, 0))
        out_spec = pl.BlockSpec((None, tile, cp), lambda l, i: (l, i, 0))
        grid, shape = (nl, r // tile), (nl, r, cp)
    else:
        rp = r + (-r % mult)
        in_spec = pl.BlockSpec((None, r, tile), lambda l, i: (l, 0, i))
        out_spec = pl.BlockSpec((None, rp, tile), lambda l, i: (l, 0, i))
        grid, shape = (nl, c // tile), (nl, rp, c)
    return pl.pallas_call(
        _cast_pad_body,
        grid=grid,
        in_specs=[in_spec],
        out_specs=out_spec,
        out_shape=jax.ShapeDtypeStruct(shape, BF16),
        compiler_params=_params("parallel", "parallel"),
        name="cast_pad",
    )(w)


def _prep_ffn(wg, wu, wd):
    return _cast_pad(wg, 2, FFN_TF), _cast_pad(wu, 2, FFN_TF), _cast_pad(wd, 1, FFN_TF)


def _trunk(x, p, n_layers):
    b, s, d = x.shape
    cos_t, sin_t = _dft_tables(s)

    def seq(a):
        return a.reshape(b, s, a.shape[-1])

    def layer_fn(li, x2):
        layer = jnp.full((1,), li, jnp.int32)
        x2 = _ffn(layer, x2, p["ln_ffn1"], *p["ffn1"])
        v1, v2, glu, sb, sg, *qkv = _in_proj(layer, x2, p["ln_mix"], p["w_in"], p["fold"])
        ya = _dft(cos_t, sin_t, seq(v1), seq(v2))
        yb, yd = _conv(layer, seq(glu), seq(sg), seq(sb), p["conv_b_w"], p["conv_b_bias"],
                       p["ln_conv_gain"], p["ln_conv_bias"], p["conv_d_w"])
        flat = lambda a: a.reshape(b * s, a.shape[-1])
        x2 = _out_proj(layer, x2, flat(ya), flat(yb), flat(yd), _attn_branches(qkv, b, s), p["w_out"], s)
        return _ffn(layer, x2, p["ln_ffn2"], *p["ffn2"], final_gain=p["ln_final"], last_layer=n_layers - 1)

    x2 = x.reshape(b * s, d)
    for li in range(n_layers):
        x2 = layer_fn(li, x2)
    return x2.reshape(b, s, d)


def _prep_params(ln_ffn1, w_ffn1_gate, w_ffn1_up, w_ffn1_down, ln_mix, w_in, w_fourier, conv_b_w,
                 conv_b_bias, ln_conv_gain, ln_conv_bias, conv_d_w, w_out, ln_ffn2, w_ffn2_gate,
                 w_ffn2_up, w_ffn2_down, ln_final):
    row = lambda a: a[:, None, :]
    return dict(
        ln_ffn1=row(ln_ffn1), ffn1=_prep_ffn(w_ffn1_gate, w_ffn1_up, w_ffn1_down),
        ln_mix=row(ln_mix), w_in=w_in.astype(BF16),
        fold=_fold_fourier(w_fourier), conv_b_w=conv_b_w, conv_b_bias=row(conv_b_bias),
        ln_conv_gain=row(ln_conv_gain), ln_conv_bias=row(ln_conv_bias), conv_d_w=conv_d_w,
        w_out=w_out.astype(BF16), ln_ffn2=row(ln_ffn2),
        ffn2=_prep_ffn(w_ffn2_gate, w_ffn2_up, w_ffn2_down), ln_final=ln_final[None, :])


def kernel(x_prompt, x_sample, ln_ffn1, w_ffn1_gate, w_ffn1_up, w_ffn1_down, ln_mix, w_in, w_fourier,
           conv_b_w, conv_b_bias, ln_conv_gain, ln_conv_bias, conv_d_w, w_out, ln_ffn2, w_ffn2_gate,
           w_ffn2_up, w_ffn2_down, ln_final):
    p = _prep_params(ln_ffn1, w_ffn1_gate, w_ffn1_up, w_ffn1_down, ln_mix, w_in, w_fourier, conv_b_w,
                     conv_b_bias, ln_conv_gain, ln_conv_bias, conv_d_w, w_out, ln_ffn2, w_ffn2_gate,
                     w_ffn2_up, w_ffn2_down, ln_final)
    n_layers = w_in.shape[0]
    return _trunk(x_prompt, p, n_layers), _trunk(x_sample, p, n_layers)
```

```python
import functools
import math

import jax
import jax.numpy as jnp
from jax import lax
from jax.experimental import pallas as pl
from jax.experimental.pallas import tpu as pltpu

F32 = jnp.float32
BF16 = jnp.bfloat16

N_FOURIER_GROUPS = 4
CONF_KERNEL = 31
SHORT_KERNEL = 3
HEAD_DIM = 64
DILATED_CONFIGS = ((128, 1), (512, 4), (2048, 16))
RMS_EPS = 1e-6
LN_EPS = 1e-5
NEG_BIG = -1e30

LANES = 128
SUBLANES_F32 = 8
SUBLANES_BF16 = 16
VMEM_LIMIT_BYTES = 60 * 1024 * 1024

FFN_TM = 1024
FFN_TF = 512
PROJ_TM = 512
DFT_TM = 1024
DFT_TK = 2048
DFT_BB = 2
DFT_GEN_ROWS = 128
CONV_TS = 512
CONV_HALO = 16
CONV_RC = 64
ATT_TQ = 1024
ATT_SUB = 128
CAST_TILE = 256


def _params(*sem):
    return pltpu.CompilerParams(dimension_semantics=sem, vmem_limit_bytes=VMEM_LIMIT_BYTES)


def _rms(x, gain):
    y = x * lax.rsqrt(jnp.mean(x * x, axis=-1, keepdims=True) + RMS_EPS)
    return y * gain


def _sigmoid(x):
    return 1.0 / (1.0 + jnp.exp(-x))


def _ffn_body(layer_ref, x_ref, gain_ref, wg_ref, wu_ref, wd_ref, *rest, last_layer):
    final_ref = rest[0] if last_layer is not None else None
    o_ref, xn_ref = rest[-2:]
    j = pl.program_id(1)

    def tile_contribution():
        xn = xn_ref[...]
        g = jnp.dot(xn, wg_ref[...], preferred_element_type=F32)
        u = jnp.dot(xn, wu_ref[...], preferred_element_type=F32)
        h = (g * _sigmoid(g)) * u * 0.5
        return jnp.dot(h.astype(BF16), wd_ref[...], preferred_element_type=F32)

    @pl.when(j == 0)
    def _():
        xn_ref[...] = _rms(x_ref[...], gain_ref[...]).astype(BF16)
        o_ref[...] = x_ref[...] + tile_contribution()

    @pl.when(j > 0)
    def _():
        o_ref[...] += tile_contribution()

    if last_layer is not None:
        @pl.when(jnp.logical_and(j == pl.num_programs(1) - 1, layer_ref[0] == last_layer))
        def _():
            o_ref[...] = _rms(o_ref[...], final_ref[...])


def _ffn(layer, x, gain, wg, wu, wd, final_gain=None, last_layer=None):
    t, d = x.shape
    tf = FFN_TF
    nf = wd.shape[1] // tf
    tm = FFN_TM
    up = pl.BlockSpec((None, d, tf), lambda i, j, l: (l[0], 0, j))
    in_specs = [
        pl.BlockSpec((tm, d), lambda i, j, l: (i, 0)),
        pl.BlockSpec((None, 1, d), lambda i, j, l: (l[0], 0, 0)),
        up, up,
        pl.BlockSpec((None, tf, d), lambda i, j, l: (l[0], j, 0)),
    ]
    args = [layer, x, gain, wg, wu, wd]
    if final_gain is not None:
        in_specs.append(pl.BlockSpec((1, d), lambda i, j, l: (0, 0)))
        args.append(final_gain)
    return pl.pallas_call(
        functools.partial(_ffn_body, last_layer=None if final_gain is None else last_layer),
        grid_spec=pltpu.PrefetchScalarGridSpec(
            num_scalar_prefetch=1,
            grid=(t // tm, nf),
            in_specs=in_specs,
            out_specs=pl.BlockSpec((tm, d), lambda i, j, l: (i, 0)),
            scratch_shapes=[pltpu.VMEM((tm, d), BF16)],
        ),
        out_shape=jax.ShapeDtypeStruct((t, d), F32),
        compiler_params=_params("parallel", "arbitrary"),
        name="ffn",
    )(*args)


def _fold_body(w_ref, o_ref):
    n = w_ref.shape[0]
    r = lax.broadcasted_iota(jnp.int32, (n, n), 0)
    c = lax.broadcasted_iota(jnp.int32, (n, n), 1)
    k = (r * c) & (n - 1)
    k = jnp.where(k >= n // 2, k - n, k)
    ang = k.astype(F32) * (2.0 * math.pi / n)
    w = w_ref[...]
    scale = n ** -0.5
    a = jnp.dot(jnp.cos(ang), w, preferred_element_type=F32, precision=lax.Precision.HIGHEST)
    b = jnp.dot(jnp.sin(ang), w, preferred_element_type=F32, precision=lax.Precision.HIGHEST)
    o_ref[:, :n] = (a * scale).astype(BF16)
    o_ref[:, n:] = (b * -scale).astype(BF16)


def _fold_fourier(w_fourier):
    nl, ng, c, _ = w_fourier.shape
    return pl.pallas_call(
        _fold_body,
        grid=(nl, ng),
        in_specs=[pl.BlockSpec((None, None, c, c), lambda l, g: (l, g, 0, 0))],
        out_specs=pl.BlockSpec((None, None, c, 2 * c), lambda l, g: (l, g, 0, 0)),
        out_shape=jax.ShapeDtypeStruct((nl, ng, c, 2 * c), BF16),
        compiler_params=_params("parallel", "parallel"),
        name="fold_fourier",
    )(w_fourier)


def _dft_gen_body(cos_ref, sin_ref, cb_ref, sb_ref, *, s, rows):
    i = pl.program_id(0)

    def angle(prod):
        k = prod & (s - 1)
        k = jnp.where(k >= s // 2, k - s, k)
        return k.astype(F32) * (2.0 * math.pi / s)

    @pl.when(i == 0)
    def _():
        r = lax.broadcasted_iota(jnp.int32, (rows, s), 0)
        c = lax.broadcasted_iota(jnp.int32, (rows, s), 1)
        b = angle(r * c)
        cb_ref[...] = jnp.cos(b)
        sb_ref[...] = jnp.sin(b)

    a = angle((i * rows) * lax.broadcasted_iota(jnp.int32, (1, s), 1))
    ca = jnp.cos(a)
    sa = jnp.sin(a)
    cb = cb_ref[...]
    sb = sb_ref[...]
    cos_ref[...] = (ca * cb - sa * sb).astype(BF16)
    sin_ref[...] = (sa * cb + ca * sb).astype(BF16)


def _dft_tables(s):
    assert s & (s - 1) == 0 and (s - 1) ** 2 < 2 ** 31
    rows = DFT_GEN_ROWS
    spec = pl.BlockSpec((rows, s), lambda i: (i, 0))
    return pl.pallas_call(
        functools.partial(_dft_gen_body, s=s, rows=rows),
        grid=(s // rows,),
        in_specs=[],
        out_specs=[spec, spec],
        out_shape=[jax.ShapeDtypeStruct((s, s), BF16)] * 2,
        scratch_shapes=[pltpu.VMEM((rows, s), F32)] * 2,
        compiler_params=_params("arbitrary"),
        name="dft_tables",
    )()


def _dft_body(c_ref, s_ref, v1_ref, v2_ref, o_ref, acc_ref, *, bb, scale):
    kk = pl.program_id(2)

    @pl.when(kk == 0)
    def _():
        acc_ref[...] = jnp.zeros_like(acc_ref)

    for b in range(bb):
        acc_ref[b] += (jnp.dot(c_ref[...], v1_ref[b], preferred_element_type=F32)
                       + jnp.dot(s_ref[...], v2_ref[b], preferred_element_type=F32))

    @pl.when(kk == pl.num_programs(2) - 1)
    def _():
        o_ref[...] = (acc_ref[...] * scale).astype(BF16)


def _dft(cos_t, sin_t, v1, v2):
    b, s, w = v1.shape
    bb, tm, tk = DFT_BB, min(DFT_TM, s), min(DFT_TK, s)
    tab = pl.BlockSpec((tm, tk), lambda n, i, k: (i, k))
    vin = pl.BlockSpec((bb, tk, w), lambda n, i, k: (n, k, 0))
    return pl.pallas_call(
        functools.partial(_dft_body, bb=bb, scale=s ** -0.5),
        grid=(b // bb, s // tm, s // tk),
        in_specs=[tab, tab, vin, vin],
        out_specs=pl.BlockSpec((bb, tm, w), lambda n, i, k: (n, i, 0)),
        out_shape=jax.ShapeDtypeStruct((b, s, w), BF16),
        scratch_shapes=[pltpu.VMEM((bb, tm, w), F32)],
        compiler_params=_params("parallel", "parallel", "arbitrary"),
        name="dft",
    )(cos_t, sin_t, v1, v2)


def _in_proj_body(layer_ref, x_ref, gain_ref, w_ref, fold_ref, v1_ref, v2_ref, glu_ref, sb_ref, sg_ref,
                  *rest, w, ng, dils):
    qkv_refs, pbuf = rest[:-1], rest[-1]
    xn = _rms(x_ref[...], gain_ref[...]).astype(BF16)

    def proj(g):
        return jnp.dot(xn, w_ref[:, g * w:(g + 1) * w], preferred_element_type=F32)

    ua = proj(0).astype(BF16)
    c = w // ng
    for g in range(ng):
        vv = jnp.dot(ua[:, g * c:(g + 1) * c], fold_ref[g], preferred_element_type=F32)
        v1_ref[:, g * c:(g + 1) * c] = vv[:, :c].astype(BF16)
        v2_ref[:, g * c:(g + 1) * c] = vv[:, c:].astype(BF16)
    glu_ref[...] = proj(1) * _sigmoid(proj(2))
    tm = x_ref.shape[0]
    for n, scale in enumerate((HEAD_DIM ** -0.5, 1.0, 1.0)):
        val = proj(3 + n) * scale
        for c in range(w // LANES):
            pbuf[0, n, c] = val[:, c * LANES:(c + 1) * LANES]
        for m, dil in enumerate(dils):
            out = qkv_refs[n * len(dils) + m]
            if m == 0:
                out[...] = val.astype(BF16)
                continue
            coarse = dils[m - 1]
            ratio, n_prev, n_cur = dil // coarse, tm // coarse, tm // dil
            for r_prev in range(coarse):
                for q in range(ratio):
                    r = r_prev + coarse * q
                    for c in range(w // LANES):
                        rows = pbuf[m - 1, n, c, pl.ds(r_prev * n_prev + q, n_cur, stride=ratio), :]
                        out[:, r * w + c * LANES:r * w + (c + 1) * LANES] = rows.astype(BF16)
                        if m + 1 < len(dils):
                            pbuf[m, n, c, r * n_cur:(r + 1) * n_cur, :] = rows
    sb_ref[...] = proj(6)
    sg_ref[...] = proj(7) * proj(8)


def _in_proj(layer, x, gain, w_in, fold):
    t, d = x.shape
    ng, c = fold.shape[1], fold.shape[2]
    w = ng * c
    tm = PROJ_TM
    dils = tuple(dil for _, dil in DILATED_CONFIGS)
    assert dils[0] == 1 and all(tm % (dil * SUBLANES_BF16) == 0 for dil in dils)
    assert all(fine % coarse == 0 for coarse, fine in zip(dils, dils[1:]))
    out = pl.BlockSpec((tm, w), lambda i, l: (i, 0))
    dts = (BF16, BF16, F32, F32, F32)
    views = [(tm // dil, dil * w) for _ in range(3) for dil in dils]
    return pl.pallas_call(
        functools.partial(_in_proj_body, w=w, ng=ng, dils=dils),
        grid_spec=pltpu.PrefetchScalarGridSpec(
            num_scalar_prefetch=1,
            grid=(t // tm,),
            in_specs=[
                pl.BlockSpec((tm, d), lambda i, l: (i, 0)),
                pl.BlockSpec((None, 1, d), lambda i, l: (l[0], 0, 0)),
                pl.BlockSpec((None, d, w_in.shape[2]), lambda i, l: (l[0], 0, 0),
                             pipeline_mode=pl.Buffered(1)),
                pl.BlockSpec((None, ng, c, 2 * c), lambda i, l: (l[0], 0, 0, 0)),
            ],
            out_specs=[out] * len(dts) + [pl.BlockSpec(blk, lambda i, l: (i, 0)) for blk in views],
            scratch_shapes=[pltpu.VMEM((len(dils) - 1, 3, w // LANES, tm, LANES), F32)],
        ),
        out_shape=([jax.ShapeDtypeStruct((t, w), dt) for dt in dts]
                   + [jax.ShapeDtypeStruct((t // tm * blk[0], blk[1]), BF16) for blk in views]),
        compiler_params=_params("parallel"),
        name="in_proj",
    )(layer, x, gain, w_in, fold)


def _conv_body(layer_ref, h_ref, hp_ref, hn_ref, g_ref, gp_ref, gn_ref, sb_ref,
               wb_ref, bias_ref, lng_ref, lnb_ref, wd_ref, yb_ref, yd_ref, hbuf, gbuf, hsh, gsh, *, ts):
    i = pl.program_id(1)
    first = i == 0
    last = i == pl.num_programs(1) - 1
    halo = CONV_HALO
    for buf, prev, cur, nxt in ((hbuf, hp_ref, h_ref, hn_ref), (gbuf, gp_ref, g_ref, gn_ref)):
        buf[0:halo] = jnp.where(first, 0.0, prev[...])
        buf[halo:halo + ts] = cur[...]
        buf[halo + ts:] = jnp.where(last, 0.0, nxt[...])

    kb, kd = CONF_KERNEL, SHORT_KERNEL
    span = ts + 2 * halo - SUBLANES_F32
    h_shifts = tuple(range(1, SUBLANES_F32))
    g_shifts = tuple(sorted({(halo - kd // 2 + k) % SUBLANES_F32 for k in range(kd)} - {0}))
    for buf, shifted, shifts in ((hbuf, hsh, h_shifts), (gbuf, gsh, g_shifts)):
        for j, sh in enumerate(shifts):
            shifted[j, 0:span] = buf[sh:sh + span]

    def window(buf, shifted, shifts, start):
        sh = start % SUBLANES_F32
        if sh == 0:
            return buf[start:start + CONV_RC, :]
        return shifted[shifts.index(sh), start - sh:start - sh + CONV_RC, :]

    for r0 in range(0, ts, CONV_RC):
        rows = slice(r0, r0 + CONV_RC)
        acc = jnp.zeros((CONV_RC, h_ref.shape[-1]), F32) + bias_ref[...]
        for k in range(kb):
            acc = acc + wb_ref[k:k + 1, :] * window(hbuf, hsh, h_shifts, r0 + halo - kb // 2 + k)
        mu = jnp.mean(acc, axis=-1, keepdims=True)
        cen = acc - mu
        var = jnp.mean(cen * cen, axis=-1, keepdims=True)
        y = cen * lax.rsqrt(var + LN_EPS) * lng_ref[...] + lnb_ref[...]
        yb_ref[rows, :] = (y * _sigmoid(y)).astype(BF16)

        acd = jnp.zeros((CONV_RC, g_ref.shape[-1]), F32)
        for k in range(kd):
            acd = acd + wd_ref[k:k + 1, :] * window(gbuf, gsh, g_shifts, r0 + halo - kd // 2 + k)
        yd_ref[rows, :] = (sb_ref[rows, :] * acd).astype(BF16)


def _conv(layer, glu, sg, sb, conv_b_w, conv_b_bias, ln_g, ln_b, conv_d_w):
    b, s, w = glu.shape
    ts, halo = CONV_TS, CONV_HALO
    nh = ts // halo
    main = pl.BlockSpec((None, ts, w), lambda n, i, l: (n, i, 0))
    prev = pl.BlockSpec((None, halo, w), lambda n, i, l: (n, jnp.maximum(i * nh - 1, 0), 0))
    nxt = pl.BlockSpec((None, halo, w), lambda n, i, l: (n, jnp.minimum((i + 1) * nh, s // halo - 1), 0))

    def per_layer(a):
        return pl.BlockSpec((None,) + a.shape[1:], lambda n, i, l: (l[0], 0, 0))

    return pl.pallas_call(
        functools.partial(_conv_body, ts=ts),
        grid_spec=pltpu.PrefetchScalarGridSpec(
            num_scalar_prefetch=1,
            grid=(b, s // ts),
            in_specs=[main, prev, nxt, main, prev, nxt, main,
                      per_layer(conv_b_w), per_layer(conv_b_bias), per_layer(ln_g), per_layer(ln_b),
                      per_layer(conv_d_w)],
            out_specs=[main, main],
            scratch_shapes=[pltpu.VMEM((ts + 2 * halo, w), F32)] * 2 + [
                pltpu.VMEM((SUBLANES_F32 - 1, ts + 2 * halo, w), F32),
                pltpu.VMEM((min(SHORT_KERNEL, SUBLANES_F32) - 1, ts + 2 * halo, w), F32)],
        ),
        out_shape=[jax.ShapeDtypeStruct((b, s, w), BF16)] * 2,
        compiler_params=_params("parallel", "parallel"),
        name="conv",
    )(layer, glu, glu, glu, sg, sg, sg, sb, conv_b_w, conv_b_bias, ln_g, ln_b, conv_d_w)


def _band_attn_body(q_ref, kp_ref, k_ref, kn_ref, vp_ref, v_ref, vn_ref, acc_ref, st_ref, kbuf, vbuf,
                    *, tq, sub, half, dil, nl, nh, nc):
    i = pl.program_id(1)
    for buf, prev, cur, nxt in ((kbuf, kp_ref, k_ref, kn_ref), (vbuf, vp_ref, v_ref, vn_ref)):
        buf[0:half] = prev[...]
        buf[half:half + tq] = cur[...]
        buf[half + tq:] = nxt[...]

    win = sub + 2 * half
    e = HEAD_DIM
    w = nh * e
    lane = lax.broadcasted_iota(jnp.int32, (sub, 2 * e), 1)
    st_lane = lax.broadcasted_iota(jnp.int32, (sub, st_ref.shape[-1]), 1)
    for so in range(0, tq, sub):
        r = lax.broadcasted_iota(jnp.int32, (sub, win), 0)
        c = lax.broadcasted_iota(jnp.int32, (sub, win), 1)
        ad = jnp.abs(r + half - c)
        key_row = i * tq + (so - half) + c
        valid = jnp.logical_and(ad <= half, jnp.logical_and(key_row >= 0, key_row < nl))
        dist = (ad * dil).astype(F32)
        for cls in range(nc):
            res_class = pl.program_id(2) * nc + cls
            stats = jnp.zeros(st_lane.shape, F32)
            for hp in range(nh // 2):
                lanes = slice(cls * w + hp * 2 * e, cls * w + (hp + 1) * 2 * e)
                qp = q_ref[so:so + sub, lanes]
                kw = kbuf[so:so + win, lanes]
                vw = vbuf[so:so + win, lanes]
                res = None
                for hh in range(2):
                    h = 2 * hp + hh
                    mine = (lane < e) if hh == 0 else (lane >= e)
                    qh = jnp.where(mine, qp, jnp.zeros_like(qp))
                    s = lax.dot_general(qh, kw, (((1,), (1,)), ((), ())), preferred_element_type=F32)
                    slope = 2.0 ** (-8.0 * (h + 1) / nh)
                    s = jnp.where(valid, s - slope * dist, NEG_BIG)
                    m = jnp.max(s, axis=-1, keepdims=True)
                    p = jnp.exp(s - m)
                    den = jnp.sum(p, axis=-1, keepdims=True)
                    pv = jnp.dot(p.astype(BF16), vw, preferred_element_type=F32)
                    res = pv if res is None else jnp.where(mine, pv, res)
                    stats = jnp.where(st_lane == h, m, stats)
                    stats = jnp.where(st_lane == nh + h, den, stats)
                if dil == 1:
                    acc_ref[hp, so:so + sub, :] = res
                else:
                    acc_ref[hp, pl.ds(so * dil + res_class, sub, stride=dil), :] = res
            if dil == 1:
                st_ref[so:so + sub, :] = stats
            else:
                st_ref[pl.ds(so * dil + res_class, sub, stride=dil), :] = stats


def _band_attn(q, k, v, b, s, window, dil):
    w = q.shape[1] // dil
    half = window // (2 * dil)
    nl = s // dil
    assert s % dil == 0 and nl % half == 0 and half % SUBLANES_BF16 == 0
    tq = min(ATT_TQ, nl)
    sub = min(ATT_SUB, tq)
    nc = min(dil, ATT_TQ // tq)
    nh = w // HEAD_DIM
    hb = tq // half
    view = lambda a: a.reshape(b, nl, dil * w)
    main = pl.BlockSpec((None, tq, nc * w), lambda n, i, r: (n, i, r))
    prev = pl.BlockSpec((None, half, nc * w), lambda n, i, r: (n, jnp.maximum(i * hb - 1, 0), r))
    nxt = pl.BlockSpec((None, half, nc * w),
                       lambda n, i, r: (n, jnp.minimum((i + 1) * hb, nl // half - 1), r))
    acc, st = pl.pallas_call(
        functools.partial(_band_attn_body, tq=tq, sub=sub, half=half, dil=dil, nl=nl, nh=nh, nc=nc),
        grid=(b, nl // tq, dil // nc),
        in_specs=[main, prev, main, nxt, prev, main, nxt],
        out_specs=[pl.BlockSpec((None, nh // 2, tq * dil, 2 * HEAD_DIM), lambda n, i, r: (n, 0, i, 0)),
                   pl.BlockSpec((None, tq * dil, LANES), lambda n, i, r: (n, i, 0))],
        out_shape=[jax.ShapeDtypeStruct((b, nh // 2, s, 2 * HEAD_DIM), F32),
                   jax.ShapeDtypeStruct((b, s, LANES), F32)],
        scratch_shapes=[pltpu.VMEM((tq + 2 * half, nc * w), BF16)] * 2,
        compiler_params=_params("parallel", "parallel", "arbitrary"),
        name="band_attn",
    )(view(q), view(k), view(k), view(k), view(v), view(v), view(v))
    return acc, st


def _merge_branches(accs, st_refs, yc_ref, nh):
    sts = [st[...] for st in st_refs]
    nlane = sts[0].shape[-1]
    lane = lax.broadcasted_iota(jnp.int32, sts[0].shape, 1)
    m_all = functools.reduce(jnp.maximum, sts)
    ws = [jnp.exp(st - m_all) for st in sts]
    dens = [pltpu.roll(st, nlane - nh, 1) for st in sts]
    total = sum(wt * dn for wt, dn in zip(ws, dens))
    w_out = yc_ref.shape[-1]
    spread = (lax.broadcasted_iota(jnp.int32, (nlane, w_out), 0)
              == lax.broadcasted_iota(jnp.int32, (nlane, w_out), 1) // HEAD_DIM).astype(BF16)
    wide = []
    for wt in ws:
        wn = jnp.where(lane < nh, wt / total, 0.0)
        hi = wn.astype(BF16)
        lo = (wn - hi.astype(F32)).astype(BF16)
        wide.append(jnp.dot(hi, spread, preferred_element_type=F32)
                    + jnp.dot(lo, spread, preferred_element_type=F32))
    pw = accs[0].shape[-1]
    for hp in range(accs[0].shape[0]):
        cols = slice(hp * pw, (hp + 1) * pw)
        out = sum(acc[hp] * wd[:, cols] for acc, wd in zip(accs, wide))
        yc_ref[:, cols] = out.astype(BF16)


def _attn_branches(qkv, b, s):
    nd = len(DILATED_CONFIGS)
    return [_band_attn(qkv[m], qkv[nd + m], qkv[2 * nd + m], b, s, window, dil)
            for m, (window, dil) in enumerate(DILATED_CONFIGS)]


def _out_proj_body(layer_ref, x_ref, ya_ref, yb_ref, yd_ref, *rest, w, nb):
    accs, st_refs = rest[:nb], rest[nb:2 * nb]
    w_ref, o_ref, yc_ref = rest[2 * nb:]
    _merge_branches(accs, st_refs, yc_ref, w // HEAD_DIM)
    acc = x_ref[...]
    for g, y_ref in enumerate((ya_ref, yb_ref, yc_ref, yd_ref)):
        acc = acc + jnp.dot(y_ref[...], w_ref[g * w:(g + 1) * w, :], preferred_element_type=F32)
    o_ref[...] = acc


def _out_proj(layer, x, ya, yb, yd, branches, w_out, s):
    t, d = x.shape
    w = ya.shape[1]
    tm = PROJ_TM
    spt = s // tm
    accs = [br[0] for br in branches]
    sts = [br[1] for br in branches]
    _, npair, _, pw = accs[0].shape
    xs = pl.BlockSpec((tm, d), lambda i, l: (i, 0))
    ys = pl.BlockSpec((tm, w), lambda i, l: (i, 0))
    a_spec = pl.BlockSpec((None, npair, tm, pw), lambda i, l: (i // spt, 0, i % spt, 0))
    s_spec = pl.BlockSpec((None, tm, LANES), lambda i, l: (i // spt, i % spt, 0))
    return pl.pallas_call(
        functools.partial(_out_proj_body, w=w, nb=len(accs)),
        grid_spec=pltpu.PrefetchScalarGridSpec(
            num_scalar_prefetch=1,
            grid=(t // tm,),
            in_specs=[xs, ys, ys, ys] + [a_spec] * len(accs) + [s_spec] * len(sts) + [
                pl.BlockSpec((None,) + w_out.shape[1:], lambda i, l: (l[0], 0, 0),
                             pipeline_mode=pl.Buffered(1))],
            out_specs=xs,
            scratch_shapes=[pltpu.VMEM((tm, w), BF16)],
        ),
        out_shape=jax.ShapeDtypeStruct((t, d), F32),
        compiler_params=_params("parallel"),
        name="out_proj",
    )(layer, x, ya, yb, yd, *accs, *sts, w_out)


def _cast_pad_body(w_ref, o_ref):
    r, c = w_ref.shape
    o_ref[:r, :c] = w_ref[...].astype(BF16)
    if o_ref.shape[0] > r:
        o_ref[r:, :] = jnp.zeros((o_ref.shape[0] - r, o_ref.shape[1]), BF16)
    if o_ref.shape[1] > c:
        o_ref[:, c:] = jnp.zeros((o_ref.shape[0], o_ref.shape[1] - c), BF16)


def _cast_pad(w, axis, mult):
    nl, r, c = w.shape
    tile = CAST_TILE
    if axis == 2:
        cp = c + (-c % mult)
        in_spec = pl.BlockSpec((None, tile, c), lambda l, i: (l, i, 0))
        out_spec = pl.BlockSpec((None, tile, cp), lambda l, i: (l, i, 0))
        grid, shape = (nl, r // tile), (nl, r, cp)
    else:
        rp = r + (-r % mult)
        in_spec = pl.BlockSpec((None, r, tile), lambda l, i: (l, 0, i))
        out_spec = pl.BlockSpec((None, rp, tile), lambda l, i: (l, 0, i))
        grid, shape = (nl, c // tile), (nl, rp, c)
    return pl.pallas_call(
        _cast_pad_body,
        grid=grid,
        in_specs=[in_spec],
        out_specs=out_spec,
        out_shape=jax.ShapeDtypeStruct(shape, BF16),
        compiler_params=_params("parallel", "parallel"),
        name="cast_pad",
    )(w)


def _prep_ffn(wg, wu, wd):
    return _cast_pad(wg, 2, FFN_TF), _cast_pad(wu, 2, FFN_TF), _cast_pad(wd, 1, FFN_TF)


def _trunk(x, p, n_layers):
    b, s, d = x.shape
    cos_t, sin_t = _dft_tables(s)

    def seq(a):
        return a.reshape(b, s, a.shape[-1])

    def layer_fn(li, x2):
        layer = jnp.full((1,), li, jnp.int32)
        x2 = _ffn(layer, x2, p["ln_ffn1"], *p["ffn1"])
        v1, v2, glu, sb, sg, *qkv = _in_proj(layer, x2, p["ln_mix"], p["w_in"], p["fold"])
        ya = _dft(cos_t, sin_t, seq(v1), seq(v2))
        yb, yd = _conv(layer, seq(glu), seq(sg), seq(sb), p["conv_b_w"], p["conv_b_bias"],
                       p["ln_conv_gain"], p["ln_conv_bias"], p["conv_d_w"])
        flat = lambda a: a.reshape(b * s, a.shape[-1])
        x2 = _out_proj(layer, x2, flat(ya), flat(yb), flat(yd), _attn_branches(qkv, b, s), p["w_out"], s)
        return _ffn(layer, x2, p["ln_ffn2"], *p["ffn2"], final_gain=p["ln_final"], last_layer=n_layers - 1)

    x2 = x.reshape(b * s, d)
    for li in range(n_layers):
        x2 = layer_fn(li, x2)
    return x2.reshape(b, s, d)


def _prep_params(ln_ffn1, w_ffn1_gate, w_ffn1_up, w_ffn1_down, ln_mix, w_in, w_fourier, conv_b_w,
                 conv_b_bias, ln_conv_gain, ln_conv_bias, conv_d_w, w_out, ln_ffn2, w_ffn2_gate,
                 w_ffn2_up, w_ffn2_down, ln_final):
    row = lambda a: a[:, None, :]
    return dict(
        ln_ffn1=row(ln_ffn1), ffn1=_prep_ffn(w_ffn1_gate, w_ffn1_up, w_ffn1_down),
        ln_mix=row(ln_mix), w_in=w_in.astype(BF16),
        fold=_fold_fourier(w_fourier), conv_b_w=conv_b_w, conv_b_bias=row(conv_b_bias),
        ln_conv_gain=row(ln_conv_gain), ln_conv_bias=row(ln_conv_bias), conv_d_w=conv_d_w,
        w_out=w_out.astype(BF16), ln_ffn2=row(ln_ffn2),
        ffn2=_prep_ffn(w_ffn2_gate, w_ffn2_up, w_ffn2_down), ln_final=ln_final[None, :])


def kernel(x_prompt, x_sample, ln_ffn1, w_ffn1_gate, w_ffn1_up, w_ffn1_down, ln_mix, w_in, w_fourier,
           conv_b_w, conv_b_bias, ln_conv_gain, ln_conv_bias, conv_d_w, w_out, ln_ffn2, w_ffn2_gate,
           w_ffn2_up, w_ffn2_down, ln_final):
    p = _prep_params(ln_ffn1, w_ffn1_gate, w_ffn1_up, w_ffn1_down, ln_mix, w_in, w_fourier, conv_b_w,
                     conv_b_bias, ln_conv_gain, ln_conv_bias, conv_d_w, w_out, ln_ffn2, w_ffn2_gate,
                     w_ffn2_up, w_ffn2_down, ln_final)
    n_layers = w_in.shape[0]
    return _trunk(x_prompt, p, n_layers), _trunk(x_sample, p, n_layers)
```
